```python
import math
import jax, jax.numpy as jnp
from jax import lax
import numpy as np

D_MODEL = 4096
BATCH = 2
SEQ = 8192
DEPTH = 2

MIX_WIDTH = D_MODEL
S5_WIDTH = MIX_WIDTH // 2
CONV_WIDTH = MIX_WIDTH - S5_WIDTH
S5_GROUP = 16
S5_GROUPS = S5_WIDTH // S5_GROUP
S5_STATE = 64
CONV_K = 3
HEAD_DIM = 128
N_HEADS = MIX_WIDTH // HEAD_DIM
ATT_WIDTH = N_HEADS * HEAD_DIM
DILATED = ((128, 1), (512, 4), (2048, 16))
ATT_BLOCK = 128
PLE_DIM = 256
LN_EPS = 1e-5
N_EVEN = (DEPTH + 1) // 2
N_ODD = DEPTH // 2
DEEPNORM_ALPHA = (2.0 * DEPTH) ** 0.25
DEEPNORM_BETA = (8.0 * DEPTH) ** -0.25
AB_IN = 2 * S5_WIDTH + 4 * CONV_WIDTH
AT_IN = 4 * ATT_WIDTH

kernel_name = "hybrid_s5_shortconv_dilated_attn_deepnorm"


def _layer_norm(y, g, b):
    yf = y.astype(jnp.float32)
    mu = jnp.mean(yf, axis=-1, keepdims=True)
    var = jnp.mean(jnp.square(yf - mu), axis=-1, keepdims=True)
    return (yf - mu) * lax.rsqrt(var + LN_EPS) * g.astype(jnp.float32) + b.astype(jnp.float32)


def _alibi_slopes(n_heads):
    return np.exp2(-8.0 * np.arange(1, n_heads + 1) / n_heads).astype(np.float32)


def _s5_combine(e1, e2):
    a1r, a1i, b1r, b1i = e1
    a2r, a2i, b2r, b2i = e2
    ar = a2r * a1r - a2i * a1i
    ai = a2r * a1i + a2i * a1r
    br = a2r * b1r - a2i * b1i + b2r
    bi = a2r * b1i + a2i * b1r + b2i
    return (ar, ai, br, bi)


def _s5(u, lam_re, lam_im, log_step, b_re, b_im, c_re, c_im, d_skip, w_glu, b_glu):
    bsz, seq, _ = u.shape
    uf = u.astype(jnp.float32)
    ug = uf.reshape(bsz, seq, S5_GROUPS, S5_GROUP)
    a_re = jnp.minimum(lam_re.astype(jnp.float32), -1e-4)
    a_im = lam_im.astype(jnp.float32)
    dt = jnp.exp(log_step.astype(jnp.float32))[:, None]
    mag = jnp.exp(dt * a_re)
    ab_re = mag * jnp.cos(dt * a_im)
    ab_im = mag * jnp.sin(dt * a_im)
    den = a_re * a_re + a_im * a_im
    nr = ab_re - 1.0
    co_re = (nr * a_re + ab_im * a_im) / den
    co_im = (ab_im * a_re - nr * a_im) / den
    br_ = b_re.astype(jnp.float32)
    bi_ = b_im.astype(jnp.float32)
    bb_re = co_re[..., None] * br_ - co_im[..., None] * bi_
    bb_im = co_re[..., None] * bi_ + co_im[..., None] * br_
    bu_re = jnp.einsum('blgc,gpc->blgp', ug, bb_re)
    bu_im = jnp.einsum('blgc,gpc->blgp', ug, bb_im)
    shape_a = (1, seq, S5_GROUPS, S5_STATE)
    a_seq_re = jnp.broadcast_to(ab_re[None, None], shape_a)
    a_seq_im = jnp.broadcast_to(ab_im[None, None], shape_a)
    _, _, h_re, h_im = lax.associative_scan(
        _s5_combine, (a_seq_re, a_seq_im, bu_re, bu_im), axis=1)
    y = (jnp.einsum('blgp,gcp->blgc', h_re, c_re.astype(jnp.float32))
         - jnp.einsum('blgp,gcp->blgc', h_im, c_im.astype(jnp.float32)))
    y = y.reshape(bsz, seq, S5_WIDTH) + d_skip.astype(jnp.float32) * uf
    y = jax.nn.gelu(y)
    return y * jax.nn.sigmoid(y @ w_glu.astype(jnp.float32) + b_glu.astype(jnp.float32))


def _short_conv(h, w, b):
    out = lax.conv_general_dilated(
        h, w[:, None, :], window_strides=(1,), padding=((CONV_K - 1, 0),),
        dimension_numbers=('NWC', 'WIO', 'NWC'), feature_group_count=h.shape[-1])
    return out + b


def _ssm_conv_mixer(x, w_in, lam_re, lam_im, log_step, b_re, b_im, c_re, c_im,
                    d_skip, w_glu, b_glu, conv_w, conv_b, w_out):
    proj = x @ w_in
    cuts = [S5_WIDTH, 2 * S5_WIDTH, 2 * S5_WIDTH + CONV_WIDTH,
            2 * S5_WIDTH + 2 * CONV_WIDTH, 2 * S5_WIDTH + 3 * CONV_WIDTH]
    u_a, z_a, h_b, g_b, g_c, z_b = jnp.split(proj, cuts, axis=-1)
    y_a = _s5(u_a, lam_re, lam_im, log_step, b_re, b_im, c_re, c_im,
              d_skip, w_glu, b_glu).astype(x.dtype) * jax.nn.silu(z_a)
    y_b = g_b * _short_conv(g_c * h_b, conv_w, conv_b) * jax.nn.silu(z_b)
    return jnp.concatenate([y_a, y_b], axis=-1) @ w_out


def _dilated_branch(q, k, v, window, dil):
    bsz, seq, nh, dh = q.shape
    n = seq // dil
    nb = -(-n // ATT_BLOCK)
    pad = nb * ATT_BLOCK - n
    span = window // dil

    def to_blocks(t):
        t = t.reshape(bsz, n, dil, nh, dh)
        t = jnp.pad(t, ((0, 0), (0, pad), (0, 0), (0, 0), (0, 0)))
        return t.reshape(bsz, nb, ATT_BLOCK, dil, nh, dh)

    def with_prev(t):
        prev = jnp.pad(t[:, :-1], ((0, 0), (1, 0), (0, 0), (0, 0), (0, 0), (0, 0)))
        return jnp.concatenate([prev, t], axis=2)

    qb = to_blocks(q)
    kk = with_prev(to_blocks(k))
    vv = with_prev(to_blocks(v)).astype(jnp.float32)
    s = jnp.einsum('bnqchd,bnkchd->bnchqk', qb, kk,
                   preferred_element_type=jnp.float32) * (dh ** -0.5)
    qi = np.arange(ATT_BLOCK)[:, None]
    kj = np.arange(2 * ATT_BLOCK)[None, :]
    delta = qi + ATT_BLOCK - kj
    band = (delta >= 0) & (delta <= span)
    has_prev = (np.arange(nb) > 0)[:, None, None]
    valid = band[None] & (has_prev | (kj >= ATT_BLOCK)[None])
    slopes = jnp.asarray(_alibi_slopes(nh))
    bias = -slopes[:, None, None] * jnp.asarray((delta * dil).astype(np.float32))[None]
    s = s + bias[None, None, None]
    s = jnp.where(jnp.asarray(valid)[None, :, None, None], s, -jnp.inf)
    m = jnp.max(s, axis=-1, keepdims=True)
    pw = jnp.exp(s - m)
    den = jnp.sum(pw, axis=-1)
    o = jnp.einsum('bnchqk,bnkchd->bnqchd', pw, vv)
    o = o / jnp.moveaxis(den, -1, 2)[..., None]
    lse = jnp.moveaxis(m[..., 0] + jnp.log(den), -1, 2)
    o = o.reshape(bsz, nb * ATT_BLOCK, dil, nh, dh)[:, :n].reshape(bsz, seq, nh, dh)
    lse = lse.reshape(bsz, nb * ATT_BLOCK, dil, nh)[:, :n].reshape(bsz, seq, nh)
    return o, lse


def _dilated_attention_mixer(x, w_in, w_out):
    bsz, seq, _ = x.shape
    q, k, v, z = jnp.split(x @ w_in, 4, axis=-1)
    q = q.reshape(bsz, seq, N_HEADS, HEAD_DIM)
    k = k.reshape(bsz, seq, N_HEADS, HEAD_DIM)
    v = v.reshape(bsz, seq, N_HEADS, HEAD_DIM)
    outs, lses = [], []
    for window, dil in DILATED:
        o, lse = _dilated_branch(q, k, v, window, dil)
        outs.append(o)
        lses.append(lse)
    wts = jax.nn.softmax(jnp.stack(lses, axis=0), axis=0)
    o = jnp.einsum('gblh,gblhd->blhd', wts, jnp.stack(outs, axis=0))
    o = o.reshape(bsz, seq, ATT_WIDTH).astype(x.dtype) * jax.nn.silu(z)
    return o @ w_out


def setup_inputs(seed: int = 0) -> dict:
    key = jax.random.key(seed)
    ks = jax.random.split(key, 32)
    nrm = lambda k, shape: jax.random.normal(k, shape, jnp.float32)
    x = nrm(ks[0], (BATCH, SEQ, D_MODEL))
    p = nrm(ks[1], (DEPTH, BATCH, SEQ, PLE_DIM))
    ab_w_in = nrm(ks[2], (N_EVEN, D_MODEL, AB_IN)) * D_MODEL ** -0.5
    ab_lambda_re = -0.5 + 0.01 * nrm(ks[3], (N_EVEN, S5_GROUPS, S5_STATE))
    ab_lambda_im = (math.pi * jnp.arange(S5_STATE, dtype=jnp.float32)[None, None, :]
                    + 0.01 * nrm(ks[4], (N_EVEN, S5_GROUPS, S5_STATE)))
    ab_log_step = jax.random.uniform(ks[5], (N_EVEN, S5_GROUPS), jnp.float32,
                                     math.log(1e-3), math.log(1e-1))
    ab_b_re = nrm(ks[6], (N_EVEN, S5_GROUPS, S5_STATE, S5_GROUP)) * (2 * S5_GROUP) ** -0.5
    ab_b_im = nrm(ks[7], (N_EVEN, S5_GROUPS, S5_STATE, S5_GROUP)) * (2 * S5_GROUP) ** -0.5
    ab_c_re = nrm(ks[8], (N_EVEN, S5_GROUPS, S5_GROUP, S5_STATE)) * S5_STATE ** -0.5
    ab_c_im = nrm(ks[9], (N_EVEN, S5_GROUPS, S5_GROUP, S5_STATE)) * S5_STATE ** -0.5
    ab_d = nrm(ks[10], (N_EVEN, S5_WIDTH))
    ab_w_glu = nrm(ks[11], (N_EVEN, S5_WIDTH, S5_WIDTH)) * S5_WIDTH ** -0.5
    ab_b_glu = 0.01 * nrm(ks[12], (N_EVEN, S5_WIDTH))
    ab_conv_w = nrm(ks[13], (N_EVEN, CONV_K, CONV_WIDTH)) * CONV_K ** -0.5
    ab_conv_b = 0.01 * nrm(ks[14], (N_EVEN, CONV_WIDTH))
    ab_w_out = nrm(ks[15], (N_EVEN, MIX_WIDTH, D_MODEL)) * (MIX_WIDTH ** -0.5) * DEEPNORM_BETA
    at_w_in = nrm(ks[16], (N_ODD, D_MODEL, AT_IN)) * D_MODEL ** -0.5
    at_w_out = nrm(ks[17], (N_ODD, ATT_WIDTH, D_MODEL)) * (ATT_WIDTH ** -0.5) * DEEPNORM_BETA
    ln_g = 1.0 + 0.01 * nrm(ks[18], (DEPTH, D_MODEL))
    ln_b = 0.01 * nrm(ks[19], (DEPTH, D_MODEL))
    ple_w = nrm(ks[20], (DEPTH, PLE_DIM, D_MODEL)) * PLE_DIM ** -0.5
    ple_gate_w = nrm(ks[21], (DEPTH, D_MODEL, D_MODEL)) * D_MODEL ** -0.5
    ple_gate_b = 0.01 * nrm(ks[22], (DEPTH, D_MODEL))
    return {"x": x, "p": p, "ab_w_in": ab_w_in, "ab_lambda_re": ab_lambda_re,
            "ab_lambda_im": ab_lambda_im, "ab_log_step": ab_log_step,
            "ab_b_re": ab_b_re, "ab_b_im": ab_b_im, "ab_c_re": ab_c_re, "ab_c_im": ab_c_im,
            "ab_d": ab_d, "ab_w_glu": ab_w_glu, "ab_b_glu": ab_b_glu,
            "ab_conv_w": ab_conv_w, "ab_conv_b": ab_conv_b, "ab_w_out": ab_w_out,
            "at_w_in": at_w_in, "at_w_out": at_w_out, "ln_g": ln_g, "ln_b": ln_b,
            "ple_w": ple_w, "ple_gate_w": ple_gate_w, "ple_gate_b": ple_gate_b}


def reference(x, p, ab_w_in, ab_lambda_re, ab_lambda_im, ab_log_step, ab_b_re, ab_b_im,
              ab_c_re, ab_c_im, ab_d, ab_w_glu, ab_b_glu, ab_conv_w, ab_conv_b, ab_w_out,
              at_w_in, at_w_out, ln_g, ln_b, ple_w, ple_gate_w, ple_gate_b):
    h = x
    for i in range(DEPTH):
        j = i // 2
        if i % 2 == 0:
            mix = _ssm_conv_mixer(h, ab_w_in[j], ab_lambda_re[j], ab_lambda_im[j],
                                  ab_log_step[j], ab_b_re[j], ab_b_im[j], ab_c_re[j],
                                  ab_c_im[j], ab_d[j], ab_w_glu[j], ab_b_glu[j],
                                  ab_conv_w[j], ab_conv_b[j], ab_w_out[j])
        else:
            mix = _dilated_attention_mixer(h, at_w_in[j], at_w_out[j])
        h = _layer_norm(DEEPNORM_ALPHA * h + mix.astype(h.dtype), ln_g[i], ln_b[i]).astype(x.dtype)
        gate = jax.nn.sigmoid(h @ ple_gate_w[i] + ple_gate_b[i])
        h = h + gate * (p[i] @ ple_w[i])
    return h
```

```python
import functools
import math

import jax
import jax.numpy as jnp
import numpy as np
from jax import lax
from jax.experimental import pallas as pl
from jax.experimental.pallas import tpu as pltpu

F32 = jnp.float32
BF16 = jnp.bfloat16

V7X_VMEM_LIMIT_BYTES = 56 * 1024 * 1024
LANES = 128
S5_GROUP = 16
S5_STATE = 64
CHUNK = LANES
CONV_K = 3
HEAD_DIM = 128
PLE_DIM = 256
LN_EPS = 1e-5
DEPTH = 2
DEEPNORM_ALPHA = (2.0 * DEPTH) ** 0.25
DILATED = ((128, 1), (512, 4), (2048, 16))
ATT_TILE = 512
ATT_WINDOW = max(w for w, _ in DILATED)
ATT_NKT = ATT_WINDOW // ATT_TILE + 1
MASKED = -1e30


def _params(*sem):
    return pltpu.CompilerParams(dimension_semantics=sem, vmem_limit_bytes=V7X_VMEM_LIMIT_BYTES)


def _tile(n, pref):
    return pref if n % pref == 0 else n


def _mm_nn_kernel(a_ref, b_ref, o_ref):
    o_ref[...] = jnp.dot(a_ref[...], b_ref[...], preferred_element_type=F32).astype(o_ref.dtype)


def _mm_nt_kernel(a_ref, b_ref, o_ref):
    o_ref[...] = lax.dot_general(a_ref[...], b_ref[...], (((1,), (1,)), ((), ())),
                                 preferred_element_type=F32).astype(o_ref.dtype)


def matmul_nn(a, b, out_dtype, tm=1024, tn=1024):
    m, k = a.shape
    _, n = b.shape
    tm, tn = _tile(m, tm), _tile(n, tn)
    return pl.pallas_call(
        _mm_nn_kernel, grid=(m // tm, n // tn),
        in_specs=[pl.BlockSpec((tm, k), lambda i, j: (i, 0)),
                  pl.BlockSpec((k, tn), lambda i, j: (0, j))],
        out_specs=pl.BlockSpec((tm, tn), lambda i, j: (i, j)),
        out_shape=jax.ShapeDtypeStruct((m, n), out_dtype),
        compiler_params=_params("parallel", "parallel"), name="mm_nn")(a, b)


def matmul_nt(a, b, out_dtype, tm=1024, tn=1024):
    m, k = a.shape
    n, _ = b.shape
    tm, tn = _tile(m, tm), _tile(n, tn)
    return pl.pallas_call(
        _mm_nt_kernel, grid=(m // tm, n // tn),
        in_specs=[pl.BlockSpec((tm, k), lambda i, j: (i, 0)),
                  pl.BlockSpec((tn, k), lambda i, j: (j, 0))],
        out_specs=pl.BlockSpec((tm, tn), lambda i, j: (i, j)),
        out_shape=jax.ShapeDtypeStruct((m, n), out_dtype),
        compiler_params=_params("parallel", "parallel"), name="mm_nt")(a, b)


def _s5_kernel(ls_ref, laml_ref, lams_ref, bt_ref, c_ref, ct_ref, u_ref, o_ref,
               kmat_ref, m_ref, wst_re_ref, wst_im_ref, wa_ref, wb_ref,
               sre_ref, sim_ref, hre_ref, him_ref, *, batch):
    grp = S5_GROUP
    nc = u_ref.shape[1]
    nck = nc // batch
    dt = jnp.exp(ls_ref[0])
    a_re_l = jnp.minimum(laml_ref[0, 0:1, :], -1e-4)
    a_im_l = laml_ref[0, 1:2, :]
    a_re_s = jnp.minimum(lams_ref[0, :, 0:1], -1e-4)
    a_im_s = lams_ref[0, :, 1:2]

    def cpow(a_re, a_im, n):
        mag = jnp.exp(n * (dt * a_re))
        ang = n * (dt * a_im)
        return mag * jnp.cos(ang), mag * jnp.sin(ang)

    ab_re, ab_im = cpow(a_re_l, a_im_l, 1.0)
    den = a_re_l * a_re_l + a_im_l * a_im_l
    nr = ab_re - 1.0
    co_re = (nr * a_re_l + ab_im * a_im_l) / den
    co_im = (ab_im * a_re_l - nr * a_im_l) / den
    br_t, bi_t = bt_ref[0, 0], bt_ref[0, 1]
    bbr = co_re * br_t - co_im * bi_t
    bbi = co_re * bi_t + co_im * br_t
    cr, ci_ = c_ref[0, 0], c_ref[0, 1]

    g1 = jnp.concatenate([bbr[i:i + 1] * cr - bbi[i:i + 1] * ci_ for i in range(grp)], axis=0)
    g2 = jnp.concatenate([bbr[i:i + 1] * ci_ + bbi[i:i + 1] * cr for i in range(grp)], axis=0)
    tau = lax.broadcasted_iota(jnp.int32, (S5_STATE, CHUNK), 1).astype(F32)
    e_re, e_im = cpow(a_re_s, a_im_s, tau)
    kmat_ref[...] = (jnp.dot(g1, e_re, precision=lax.Precision.HIGHEST, preferred_element_type=F32)
                     - jnp.dot(g2, e_im, precision=lax.Precision.HIGHEST, preferred_element_type=F32))

    causal = (lax.broadcasted_iota(jnp.int32, (CHUNK, CHUNK), 1)
              >= lax.broadcasted_iota(jnp.int32, (CHUNK, CHUNK), 0))

    def toeplitz_rows(ci, carry):
        row0 = pl.multiple_of(ci * CHUNK, CHUNK)
        for co in range(grp):
            krow = kmat_ref[pl.ds(ci * grp + co, 1), :]
            shifted = pltpu.roll(jnp.broadcast_to(krow, (CHUNK, CHUNK)), 0, 1, stride=1, stride_axis=0)
            m_ref[pl.ds(row0, CHUNK), co * CHUNK:(co + 1) * CHUNK] = (
                jnp.where(causal, shifted, 0.0).astype(BF16))
        return carry

    lax.fori_loop(0, grp, toeplitz_rows, 0)

    rev = (CHUNK - 1 - lax.broadcasted_iota(jnp.int32, (CHUNK, S5_STATE), 0)).astype(F32)
    er_re, er_im = cpow(a_re_l, a_im_l, rev)
    for i in range(grp):
        wst_re_ref[i * CHUNK:(i + 1) * CHUNK, :] = (er_re * bbr[i:i + 1] - er_im * bbi[i:i + 1]).astype(BF16)
        wst_im_ref[i * CHUNK:(i + 1) * CHUNK, :] = (er_re * bbi[i:i + 1] + er_im * bbr[i:i + 1]).astype(BF16)

    t1 = (lax.broadcasted_iota(jnp.int32, (S5_STATE, CHUNK), 1) + 1).astype(F32)
    e1_re, e1_im = cpow(a_re_s, a_im_s, t1)
    cr_t, ci_t = ct_ref[0, 0], ct_ref[0, 1]
    for o in range(grp):
        wa_ref[:, o * CHUNK:(o + 1) * CHUNK] = (cr_t[:, o:o + 1] * e1_re - ci_t[:, o:o + 1] * e1_im).astype(BF16)
        wb_ref[:, o * CHUNK:(o + 1) * CHUNK] = (-(cr_t[:, o:o + 1] * e1_im + ci_t[:, o:o + 1] * e1_re)).astype(BF16)

    lhs = jnp.concatenate([u_ref[i].astype(BF16) for i in range(grp)], axis=1)
    sre_ref[...] = jnp.dot(lhs, wst_re_ref[...], preferred_element_type=F32)
    sim_ref[...] = jnp.dot(lhs, wst_im_ref[...], preferred_element_type=F32)

    ac_re, ac_im = cpow(a_re_l, a_im_l, float(CHUNK))

    def carry_state(k, carry):
        new = []
        for b in range(batch):
            h_re, h_im = carry[2 * b], carry[2 * b + 1]
            row = b * nck + k
            hre_ref[pl.ds(row, 1), :] = h_re
            him_ref[pl.ds(row, 1), :] = h_im
            new.append(ac_re * h_re - ac_im * h_im + sre_ref[pl.ds(row, 1), :])
            new.append(ac_re * h_im + ac_im * h_re + sim_ref[pl.ds(row, 1), :])
        return tuple(new)

    zero = jnp.zeros((1, S5_STATE), F32)
    lax.fori_loop(0, nck, carry_state, (zero,) * (2 * batch))

    y = (jnp.dot(lhs, m_ref[...], preferred_element_type=F32)
         + jnp.dot(hre_ref[...].astype(BF16), wa_ref[...], preferred_element_type=F32)
         + jnp.dot(him_ref[...].astype(BF16), wb_ref[...], preferred_element_type=F32))
    for o in range(grp):
        o_ref[o] = y[:, o * CHUNK:(o + 1) * CHUNK]


def s5_core(u_t, log_step, lam_re, lam_im, b_re, b_im, c_re, c_im, batch):
    width, tokens = u_t.shape
    groups = width // S5_GROUP
    nc = tokens // CHUNK
    u3 = u_t.reshape(width, nc, CHUNK)
    ls = log_step.reshape(groups, 1, 1)
    lam_l = jnp.stack([lam_re, lam_im], axis=1)
    lam_s = jnp.stack([lam_re, lam_im], axis=2)
    b_t = jnp.stack([b_re, b_im], axis=1).transpose(0, 1, 3, 2)
    c = jnp.stack([c_re, c_im], axis=1)
    c_t = c.transpose(0, 1, 3, 2)
    gw = S5_GROUP * CHUNK
    small = lambda *shape: pl.BlockSpec((1,) + shape, lambda g: (g,) + (0,) * len(shape))
    out = pl.pallas_call(
        functools.partial(_s5_kernel, batch=batch), grid=(groups,),
        in_specs=[small(1, 1), small(2, S5_STATE), small(S5_STATE, 2),
                  small(2, S5_GROUP, S5_STATE), small(2, S5_GROUP, S5_STATE), small(2, S5_STATE, S5_GROUP),
                  pl.BlockSpec((S5_GROUP, nc, CHUNK), lambda g: (g, 0, 0))],
        out_specs=pl.BlockSpec((S5_GROUP, nc, CHUNK), lambda g: (g, 0, 0)),
        out_shape=jax.ShapeDtypeStruct((width, nc, CHUNK), F32),
        scratch_shapes=[pltpu.VMEM((S5_GROUP * S5_GROUP, CHUNK), F32),
                        pltpu.VMEM((gw, gw), BF16),
                        pltpu.VMEM((gw, S5_STATE), BF16), pltpu.VMEM((gw, S5_STATE), BF16),
                        pltpu.VMEM((S5_STATE, gw), BF16), pltpu.VMEM((S5_STATE, gw), BF16),
                        pltpu.VMEM((nc, S5_STATE), F32), pltpu.VMEM((nc, S5_STATE), F32),
                        pltpu.VMEM((nc, S5_STATE), F32), pltpu.VMEM((nc, S5_STATE), F32)],
        compiler_params=_params("parallel"), name="s5_core")(ls, lam_l, lam_s, b_t, c, c_t, u3)
    return out.reshape(width, tokens)


def _glu_t_kernel(y_ref, u_ref, z_ref, d_ref, w_ref, b_ref, o_ref):
    y = jax.nn.gelu(y_ref[...] + d_ref[...] * u_ref[...])
    acc = jnp.dot(w_ref[...], y.astype(BF16), preferred_element_type=F32)
    z = z_ref[...]
    o_ref[...] = (y * jax.nn.sigmoid(acc + b_ref[...]) * (z * jax.nn.sigmoid(z))).astype(o_ref.dtype)


def s5_glu_t(y_t, p_t, d_skip, w_glu_t, b_glu, tm=256):
    width, tokens = y_t.shape
    tm = _tile(tokens, tm)
    col = lambda i: (0, i)
    return pl.pallas_call(
        _glu_t_kernel, grid=(tokens // tm,),
        in_specs=[pl.BlockSpec((width, tm), col), pl.BlockSpec((width, tm), col),
                  pl.BlockSpec((width, tm), lambda i: (1, i)),
                  pl.BlockSpec((width, 1), lambda i: (0, 0)),
                  pl.BlockSpec((width, width), lambda i: (0, 0)),
                  pl.BlockSpec((width, 1), lambda i: (0, 0))],
        out_specs=pl.BlockSpec((width, tm), col),
        out_shape=jax.ShapeDtypeStruct((width, tokens), BF16),
        compiler_params=_params("parallel"), name="s5_glu")(
            y_t, p_t, p_t, d_skip.reshape(width, 1), w_glu_t, b_glu.reshape(width, 1))


def _conv_kernel(hb_ref, gb_ref, gc_ref, zb_ref, hbp_ref, gcp_ref, w_ref, b_ref, o_ref, *, seq):
    tm = hb_ref.shape[0]
    v = gc_ref[...] * hb_ref[...]
    first = (pl.program_id(0) * tm) % seq == 0
    vp = jnp.where(first, 0.0, gcp_ref[...] * hbp_ref[...])
    ext = jnp.concatenate([vp, v], axis=0)
    conv = (w_ref[0:1, :] * ext[6:6 + tm] + w_ref[1:2, :] * ext[7:7 + tm] + w_ref[2:3, :] * v
            + b_ref[...])
    z = zb_ref[...]
    o_ref[...] = (gb_ref[...] * conv * (z * jax.nn.sigmoid(z))).astype(o_ref.dtype)


def short_conv_gate(pb, conv_w, conv_b, seq, tm=512, tc=512):
    tokens, w4 = pb.shape
    width = w4 // 4
    tm, tc = _tile(seq, tm), _tile(width, tc)
    ncb = width // tc
    seg = lambda s: pl.BlockSpec((tm, tc), lambda i, j: (i, s * ncb + j))
    prev = lambda s: pl.BlockSpec((8, tc), lambda i, j: (jnp.maximum(i * (tm // 8) - 1, 0), s * ncb + j))
    return pl.pallas_call(
        functools.partial(_conv_kernel, seq=seq), grid=(tokens // tm, ncb),
        in_specs=[seg(0), seg(1), seg(2), seg(3), prev(0), prev(2),
                  pl.BlockSpec((CONV_K, tc), lambda i, j: (0, j)),
                  pl.BlockSpec((1, tc), lambda i, j: (0, j))],
        out_specs=pl.BlockSpec((tm, tc), lambda i, j: (i, j)),
        out_shape=jax.ShapeDtypeStruct((tokens, width), BF16),
        compiler_params=_params("parallel", "parallel"), name="short_conv")(
            pb, pb, pb, pb, pb, pb, conv_w, conv_b.reshape(1, width))


def _out0_kernel(ya_ref, yb_ref, wa_ref, wb_ref, o_ref):
    acc = lax.dot_general(ya_ref[...], wa_ref[...], (((0,), (0,)), ((), ())), preferred_element_type=F32)
    o_ref[...] = acc + jnp.dot(yb_ref[...], wb_ref[...], preferred_element_type=F32)


def out_proj0(ya_t, yb, w_out, tm=1024, tn=1024):
    wa_rows, tokens = ya_t.shape
    wb_rows = yb.shape[1]
    n = w_out.shape[1]
    tm, tn = _tile(tokens, tm), _tile(n, tn)
    return pl.pallas_call(
        _out0_kernel, grid=(tokens // tm, n // tn),
        in_specs=[pl.BlockSpec((wa_rows, tm), lambda i, j: (0, i)),
                  pl.BlockSpec((tm, wb_rows), lambda i, j: (i, 0)),
                  pl.BlockSpec((wa_rows, tn), lambda i, j: (0, j)),
                  pl.BlockSpec((wb_rows, tn), lambda i, j: (wa_rows // wb_rows, j))],
        out_specs=pl.BlockSpec((tm, tn), lambda i, j: (i, j)),
        out_shape=jax.ShapeDtypeStruct((tokens, n), F32),
        compiler_params=_params("parallel", "parallel"), name="out_proj0")(ya_t, yb, w_out, w_out)


def _ln_kernel(h_ref, mix_ref, g_ref, b_ref, o_ref, ob_ref):
    y = DEEPNORM_ALPHA * h_ref[...] + mix_ref[...]
    mu = jnp.mean(y, axis=-1, keepdims=True)
    yc = y - mu
    var = jnp.mean(yc * yc, axis=-1, keepdims=True)
    out = yc * lax.rsqrt(var + LN_EPS) * g_ref[...] + b_ref[...]
    o_ref[...] = out
    ob_ref[...] = out.astype(BF16)


def deepnorm_ln(h, mix, g, b, tm=256):
    tokens, d = h.shape
    tm = _tile(tokens, tm)
    row = pl.BlockSpec((tm, d), lambda i: (i, 0))
    vec = pl.BlockSpec((1, d), lambda i: (0, 0))
    return pl.pallas_call(
        _ln_kernel, grid=(tokens // tm,),
        in_specs=[row, row, vec, vec], out_specs=[row, row],
        out_shape=[jax.ShapeDtypeStruct((tokens, d), F32), jax.ShapeDtypeStruct((tokens, d), BF16)],
        compiler_params=_params("parallel"), name="deepnorm_ln")(h, mix, g.reshape(1, d), b.reshape(1, d))


def _gate_kernel(hb_ref, w_ref, bg_ref, h_ref, p_ref, wp_ref, o_ref, ob_ref):
    gate = jax.nn.sigmoid(jnp.dot(hb_ref[...], w_ref[...], preferred_element_type=F32) + bg_ref[...])
    out = h_ref[...] + gate * jnp.dot(p_ref[...], wp_ref[...], preferred_element_type=F32)
    o_ref[...] = out
    ob_ref[...] = out.astype(BF16)


def ple_gate(h, hb, w_gate, b_gate, p, w_ple, tm=1024, tn=512):
    tokens, d = h.shape
    pd = p.shape[1]
    tm, tn = _tile(tokens, tm), _tile(d, tn)
    return pl.pallas_call(
        _gate_kernel, grid=(tokens // tm, d // tn),
        in_specs=[pl.BlockSpec((tm, d), lambda i, j: (i, 0)),
                  pl.BlockSpec((d, tn), lambda i, j: (0, j)),
                  pl.BlockSpec((1, tn), lambda i, j: (0, j)),
                  pl.BlockSpec((tm, tn), lambda i, j: (i, j)),
                  pl.BlockSpec((tm, pd), lambda i, j: (i, 0)),
                  pl.BlockSpec((pd, tn), lambda i, j: (0, j))],
        out_specs=[pl.BlockSpec((tm, tn), lambda i, j: (i, j))] * 2,
        out_shape=[jax.ShapeDtypeStruct((tokens, d), F32), jax.ShapeDtypeStruct((tokens, d), BF16)],
        compiler_params=_params("parallel", "parallel"), name="ple_gate")(
            hb, w_gate, b_gate.reshape(1, d), h, p, w_ple)


def _attn_kernel(sl_ref, q_ref, k_ref, v_ref, z_ref, o_ref, m_ref, l_ref, acc_ref, dist_ref, lmul_ref):
    b, h, i, kk = (pl.program_id(a) for a in range(4))
    tq = q_ref.shape[0]

    @pl.when((b == 0) & (h == 0) & (i == 0) & (kk == 0))
    def _tables():
        r = lax.broadcasted_iota(jnp.int32, (tq, tq), 0)
        c = lax.broadcasted_iota(jnp.int32, (tq, tq), 1)
        for dlt in range(ATT_NKT):
            d = r - c + tq * dlt
            mult = jnp.zeros((tq, tq), F32)
            for window, dil in DILATED:
                mult = mult + jnp.where(((d & (dil - 1)) == 0) & (d <= window), 1.0, 0.0)
            ok = (d >= 0) & (mult > 0.0)
            dist_ref[dlt] = d.astype(F32)
            lmul_ref[dlt] = jnp.where(ok, jnp.log(jnp.maximum(mult, 1.0)), MASKED)

    @pl.when(kk == 0)
    def _init():
        m_ref[...] = jnp.full(m_ref.shape, MASKED, F32)
        l_ref[...] = jnp.zeros(l_ref.shape, F32)
        acc_ref[...] = jnp.zeros(acc_ref.shape, F32)

    dlt = ATT_NKT - 1 - kk

    @pl.when(i - dlt >= 0)
    def _step():
        s = lax.dot_general(q_ref[...], k_ref[...], (((1,), (1,)), ((), ())), preferred_element_type=F32)
        s = s * (HEAD_DIM ** -0.5) + lmul_ref[dlt] - sl_ref[0] * dist_ref[dlt]
        m_prev = m_ref[...]
        m_new = jnp.maximum(m_prev, jnp.max(s, axis=-1, keepdims=True))
        alpha = jnp.exp(m_prev - m_new)
        p = jnp.exp(s - m_new)
        l_ref[...] = alpha * l_ref[...] + jnp.sum(p, axis=-1, keepdims=True)
        acc_ref[...] = alpha * acc_ref[...] + jnp.dot(p.astype(BF16), v_ref[...], preferred_element_type=F32)
        m_ref[...] = m_new

    @pl.when(kk == ATT_NKT - 1)
    def _finish():
        z = z_ref[...]
        o_ref[...] = (acc_ref[...] / l_ref[...] * (z * jax.nn.sigmoid(z))).astype(o_ref.dtype)


def dilated_attention(qkv, z, batch, seq):
    tokens, width = z.shape
    heads = width // HEAD_DIM
    tq = ATT_TILE
    nq = seq // tq
    slopes = jnp.asarray(np.exp2(-8.0 * np.arange(1, heads + 1) / heads).astype(np.float32)).reshape(heads, 1, 1)
    kv_row = lambda b, h, i, kk: b * nq + jnp.maximum(i - (ATT_NKT - 1) + kk, 0)
    return pl.pallas_call(
        _attn_kernel, grid=(batch, heads, nq, ATT_NKT),
        in_specs=[pl.BlockSpec((1, 1, 1), lambda b, h, i, kk: (h, 0, 0)),
                  pl.BlockSpec((tq, HEAD_DIM), lambda b, h, i, kk: (b * nq + i, h)),
                  pl.BlockSpec((tq, HEAD_DIM), lambda b, h, i, kk: (kv_row(b, h, i, kk), heads + h)),
                  pl.BlockSpec((tq, HEAD_DIM), lambda b, h, i, kk: (kv_row(b, h, i, kk), 2 * heads + h)),
                  pl.BlockSpec((tq, HEAD_DIM), lambda b, h, i, kk: (b * nq + i, h))],
        out_specs=pl.BlockSpec((tq, HEAD_DIM), lambda b, h, i, kk: (b * nq + i, h)),
        out_shape=jax.ShapeDtypeStruct((tokens, width), BF16),
        scratch_shapes=[pltpu.VMEM((tq, 1), F32), pltpu.VMEM((tq, 1), F32), pltpu.VMEM((tq, HEAD_DIM), F32),
                        pltpu.VMEM((ATT_NKT, tq, tq), F32), pltpu.VMEM((ATT_NKT, tq, tq), F32)],
        compiler_params=_params("arbitrary", "arbitrary", "arbitrary", "arbitrary"),
        name="dilated_attn")(slopes, qkv, qkv, qkv, z)


def kernel(x, p, ab_w_in, ab_lambda_re, ab_lambda_im, ab_log_step, ab_b_re, ab_b_im, ab_c_re, ab_c_im, ab_d,
           ab_w_glu, ab_b_glu, ab_conv_w, ab_conv_b, ab_w_out, at_w_in, at_w_out, ln_g, ln_b, ple_w,
           ple_gate_w, ple_gate_b):
    batch, seq, d = x.shape
    tokens = batch * seq
    s5w = d // 2
    assert seq % ATT_WINDOW == 0 and seq % CHUNK == 0 and s5w % S5_GROUP == 0
    h = x.reshape(tokens, d)
    hb = h.astype(BF16)
    pb = p.reshape(DEPTH, tokens, PLE_DIM).astype(BF16)

    w_in = ab_w_in[0].astype(BF16)
    proj_t = matmul_nt(w_in[:, :2 * s5w].T, hb, F32)
    proj_b = matmul_nn(hb, w_in[:, 2 * s5w:], F32)
    y_t = s5_core(proj_t[:s5w], ab_log_step[0], ab_lambda_re[0], ab_lambda_im[0], ab_b_re[0], ab_b_im[0],
                  ab_c_re[0], ab_c_im[0], batch)
    ya_t = s5_glu_t(y_t, proj_t, ab_d[0], ab_w_glu[0].T.astype(BF16), ab_b_glu[0])
    yb = short_conv_gate(proj_b, ab_conv_w[0], ab_conv_b[0], seq)
    mix = out_proj0(ya_t, yb, ab_w_out[0].astype(BF16))
    h, hb = deepnorm_ln(h, mix, ln_g[0], ln_b[0])
    h, hb = ple_gate(h, hb, ple_gate_w[0].astype(BF16), ple_gate_b[0], pb[0], ple_w[0].astype(BF16))

    w_in = at_w_in[0].astype(BF16)
    qkv = matmul_nn(hb, w_in[:, :3 * d], BF16)
    z = matmul_nn(hb, w_in[:, 3 * d:], F32)
    og = dilated_attention(qkv, z, batch, seq)
    mix = matmul_nn(og, at_w_out[0].astype(BF16), F32)
    h, hb = deepnorm_ln(h, mix, ln_g[1], ln_b[1])
    h, _ = ple_gate(h, hb, ple_gate_w[1].astype(BF16), ple_gate_b[1], pb[1], ple_w[1].astype(BF16))
    return h.reshape(batch, seq, d)
```

```python
import functools
import math

import jax
import jax.numpy as jnp
import numpy as np
from jax import lax
from jax.experimental import pallas as pl
from jax.experimental.pallas import tpu as pltpu

F32 = jnp.float32
BF16 = jnp.bfloat16

V7X_VMEM_LIMIT_BYTES = 56 * 1024 * 1024
LANES = 128
S5_GROUP = 16
S5_STATE = 64
CHUNK = LANES
CONV_K = 3
HEAD_DIM = 128
PLE_DIM = 256
LN_EPS = 1e-5
DEPTH = 2
DEEPNORM_ALPHA = (2.0 * DEPTH) ** 0.25
DILATED = ((128, 1), (512, 4), (2048, 16))
SPAN = 128
DEINT = max(d for _, d in DILATED)
ATT_GROUP_ROWS = {16: 128, 4: 32, 1: 16}
FAR = 1e32
ATT_BLOCKS_PER_STEP = {16: 4, 4: 4, 1: 1}
assert all(w // d == SPAN for w, d in DILATED)


def _params(*sem):
    return pltpu.CompilerParams(dimension_semantics=sem, vmem_limit_bytes=V7X_VMEM_LIMIT_BYTES)


def _tile(n, pref):
    return pref if n % pref == 0 else n


def _mm_nn_kernel(a_ref, b_ref, o_ref):
    o_ref[...] = jnp.dot(a_ref[...], b_ref[...], preferred_element_type=F32).astype(o_ref.dtype)


def _mm_nt_kernel(a_ref, b_ref, o_ref):
    o_ref[...] = lax.dot_general(a_ref[...], b_ref[...], (((1,), (1,)), ((), ())),
                                 preferred_element_type=F32).astype(o_ref.dtype)


def _deint_tile(i, nt):
    seq_res, ii = i // nt, i % nt
    return (seq_res // DEINT) * nt + ii, seq_res % DEINT


def matmul_nn(a, b, out_dtype, tm=1024, tn=1024, seq=None, deint=None):
    m, k = a.shape
    _, n = b.shape
    tm, tn = _tile(m, tm), _tile(n, tn)
    if deint is not None:
        n_res = seq // DEINT
        tm = _tile(n_res, min(tm, n_res))
        nt, ntn = n_res // tm, n // tn
    a_spec = pl.BlockSpec((tm, k), lambda i, j: (i, 0))
    o_spec = pl.BlockSpec((tm, tn), lambda i, j: (i, j))
    o_shape = (m, n)
    if deint == "lhs":
        a = a.reshape(m // DEINT, DEINT * k)
        a_spec = pl.BlockSpec((tm, k), lambda i, j: _deint_tile(i, nt))
    elif deint == "out":
        o_shape = (m // DEINT, DEINT * n)

        def o_map(i, j):
            row, res = _deint_tile(i, nt)
            return row, res * ntn + j
        o_spec = pl.BlockSpec((tm, tn), o_map)
    out = pl.pallas_call(
        _mm_nn_kernel, grid=(m // tm, n // tn),
        in_specs=[a_spec, pl.BlockSpec((k, tn), lambda i, j: (0, j))],
        out_specs=o_spec,
        out_shape=jax.ShapeDtypeStruct(o_shape, out_dtype),
        compiler_params=_params("parallel", "parallel"), name="mm_nn")(a, b)
    return out.reshape(m, n)


def matmul_nt(a, b, out_dtype, tm=1024, tn=1024):
    m, k = a.shape
    n, _ = b.shape
    tm, tn = _tile(m, tm), _tile(n, tn)
    return pl.pallas_call(
        _mm_nt_kernel, grid=(m // tm, n // tn),
        in_specs=[pl.BlockSpec((tm, k), lambda i, j: (i, 0)),
                  pl.BlockSpec((tn, k), lambda i, j: (j, 0))],
        out_specs=pl.BlockSpec((tm, tn), lambda i, j: (i, j)),
        out_shape=jax.ShapeDtypeStruct((m, n), out_dtype),
        compiler_params=_params("parallel", "parallel"), name="mm_nt")(a, b)


def _s5_kernel(ls_ref, laml_ref, lams_ref, bt_ref, c_ref, ct_ref, u_ref, o_ref,
               kmat_ref, m_ref, wst_re_ref, wst_im_ref, wa_ref, wb_ref,
               sre_ref, sim_ref, hre_ref, him_ref, *, batch):
    grp = S5_GROUP
    nc = u_ref.shape[1]
    nck = nc // batch
    dt = jnp.exp(ls_ref[0])
    a_re_l = jnp.minimum(laml_ref[0, 0:1, :], -1e-4)
    a_im_l = laml_ref[0, 1:2, :]
    a_re_s = jnp.minimum(lams_ref[0, :, 0:1], -1e-4)
    a_im_s = lams_ref[0, :, 1:2]

    def cpow(a_re, a_im, n):
        mag = jnp.exp(n * (dt * a_re))
        ang = n * (dt * a_im)
        return mag * jnp.cos(ang), mag * jnp.sin(ang)

    ab_re, ab_im = cpow(a_re_l, a_im_l, 1.0)
    den = a_re_l * a_re_l + a_im_l * a_im_l
    nr = ab_re - 1.0
    co_re = (nr * a_re_l + ab_im * a_im_l) / den
    co_im = (ab_im * a_re_l - nr * a_im_l) / den
    br_t, bi_t = bt_ref[0, 0], bt_ref[0, 1]
    bbr = co_re * br_t - co_im * bi_t
    bbi = co_re * bi_t + co_im * br_t
    cr, ci_ = c_ref[0, 0], c_ref[0, 1]

    g1 = jnp.concatenate([bbr[i:i + 1] * cr - bbi[i:i + 1] * ci_ for i in range(grp)], axis=0)
    g2 = jnp.concatenate([bbr[i:i + 1] * ci_ + bbi[i:i + 1] * cr for i in range(grp)], axis=0)
    tau = lax.broadcasted_iota(jnp.int32, (S5_STATE, CHUNK), 1).astype(F32)
    e_re, e_im = cpow(a_re_s, a_im_s, tau)
    kmat_ref[...] = (jnp.dot(g1, e_re, precision=lax.Precision.HIGHEST, preferred_element_type=F32)
                     - jnp.dot(g2, e_im, precision=lax.Precision.HIGHEST, preferred_element_type=F32))

    causal = (lax.broadcasted_iota(jnp.int32, (CHUNK, CHUNK), 1)
              >= lax.broadcasted_iota(jnp.int32, (CHUNK, CHUNK), 0))

    def toeplitz_rows(ci, carry):
        row0 = pl.multiple_of(ci * CHUNK, CHUNK)
        for co in range(grp):
            krow = kmat_ref[pl.ds(ci * grp + co, 1), :]
            shifted = pltpu.roll(jnp.broadcast_to(krow, (CHUNK, CHUNK)), 0, 1, stride=1, stride_axis=0)
            m_ref[pl.ds(row0, CHUNK), co * CHUNK:(co + 1) * CHUNK] = (
                jnp.where(causal, shifted, 0.0).astype(BF16))
        return carry

    lax.fori_loop(0, grp, toeplitz_rows, 0)

    rev = (CHUNK - 1 - lax.broadcasted_iota(jnp.int32, (CHUNK, S5_STATE), 0)).astype(F32)
    er_re, er_im = cpow(a_re_l, a_im_l, rev)
    for i in range(grp):
        wst_re_ref[i * CHUNK:(i + 1) * CHUNK, :] = (er_re * bbr[i:i + 1] - er_im * bbi[i:i + 1]).astype(BF16)
        wst_im_ref[i * CHUNK:(i + 1) * CHUNK, :] = (er_re * bbi[i:i + 1] + er_im * bbr[i:i + 1]).astype(BF16)

    t1 = (lax.broadcasted_iota(jnp.int32, (S5_STATE, CHUNK), 1) + 1).astype(F32)
    e1_re, e1_im = cpow(a_re_s, a_im_s, t1)
    cr_t, ci_t = ct_ref[0, 0], ct_ref[0, 1]
    for o in range(grp):
        wa_ref[:, o * CHUNK:(o + 1) * CHUNK] = (cr_t[:, o:o + 1] * e1_re - ci_t[:, o:o + 1] * e1_im).astype(BF16)
        wb_ref[:, o * CHUNK:(o + 1) * CHUNK] = (-(cr_t[:, o:o + 1] * e1_im + ci_t[:, o:o + 1] * e1_re)).astype(BF16)

    lhs = jnp.concatenate([u_ref[i].astype(BF16) for i in range(grp)], axis=1)
    sre_ref[...] = jnp.dot(lhs, wst_re_ref[...], preferred_element_type=F32)
    sim_ref[...] = jnp.dot(lhs, wst_im_ref[...], preferred_element_type=F32)

    ac_re, ac_im = cpow(a_re_l, a_im_l, float(CHUNK))

    def carry_state(k, carry):
        new = []
        for b in range(batch):
            h_re, h_im = carry[2 * b], carry[2 * b + 1]
            row = b * nck + k
            hre_ref[pl.ds(row, 1), :] = h_re
            him_ref[pl.ds(row, 1), :] = h_im
            new.append(ac_re * h_re - ac_im * h_im + sre_ref[pl.ds(row, 1), :])
            new.append(ac_re * h_im + ac_im * h_re + sim_ref[pl.ds(row, 1), :])
        return tuple(new)

    zero = jnp.zeros((1, S5_STATE), F32)
    lax.fori_loop(0, nck, carry_state, (zero,) * (2 * batch))

    y = (jnp.dot(lhs, m_ref[...], preferred_element_type=F32)
         + jnp.dot(hre_ref[...].astype(BF16), wa_ref[...], preferred_element_type=F32)
         + jnp.dot(him_ref[...].astype(BF16), wb_ref[...], preferred_element_type=F32))
    for o in range(grp):
        o_ref[o] = y[:, o * CHUNK:(o + 1) * CHUNK]


def s5_core(u_t, log_step, lam_re, lam_im, b_re, b_im, c_re, c_im, batch):
    width, tokens = u_t.shape
    groups = width // S5_GROUP
    nc = tokens // CHUNK
    u3 = u_t.reshape(width, nc, CHUNK)
    ls = log_step.reshape(groups, 1, 1)
    lam_l = jnp.stack([lam_re, lam_im], axis=1)
    lam_s = jnp.stack([lam_re, lam_im], axis=2)
    b_t = jnp.stack([b_re, b_im], axis=1).transpose(0, 1, 3, 2)
    c = jnp.stack([c_re, c_im], axis=1)
    c_t = c.transpose(0, 1, 3, 2)
    gw = S5_GROUP * CHUNK
    small = lambda *shape: pl.BlockSpec((1,) + shape, lambda g: (g,) + (0,) * len(shape))
    out = pl.pallas_call(
        functools.partial(_s5_kernel, batch=batch), grid=(groups,),
        in_specs=[small(1, 1), small(2, S5_STATE), small(S5_STATE, 2),
                  small(2, S5_GROUP, S5_STATE), small(2, S5_GROUP, S5_STATE), small(2, S5_STATE, S5_GROUP),
                  pl.BlockSpec((S5_GROUP, nc, CHUNK), lambda g: (g, 0, 0))],
        out_specs=pl.BlockSpec((S5_GROUP, nc, CHUNK), lambda g: (g, 0, 0)),
        out_shape=jax.ShapeDtypeStruct((width, nc, CHUNK), F32),
        scratch_shapes=[pltpu.VMEM((S5_GROUP * S5_GROUP, CHUNK), F32),
                        pltpu.VMEM((gw, gw), BF16),
                        pltpu.VMEM((gw, S5_STATE), BF16), pltpu.VMEM((gw, S5_STATE), BF16),
                        pltpu.VMEM((S5_STATE, gw), BF16), pltpu.VMEM((S5_STATE, gw), BF16),
                        pltpu.VMEM((nc, S5_STATE), F32), pltpu.VMEM((nc, S5_STATE), F32),
                        pltpu.VMEM((nc, S5_STATE), F32), pltpu.VMEM((nc, S5_STATE), F32)],
        compiler_params=_params("parallel"), name="s5_core")(ls, lam_l, lam_s, b_t, c, c_t, u3)
    return out.reshape(width, tokens)


def _glu_t_kernel(y_ref, u_ref, z_ref, d_ref, w_ref, b_ref, o_ref):
    y = jax.nn.gelu(y_ref[...] + d_ref[...] * u_ref[...])
    acc = jnp.dot(w_ref[...], y.astype(BF16), preferred_element_type=F32)
    z = z_ref[...]
    o_ref[...] = (y * jax.nn.sigmoid(acc + b_ref[...]) * (z * jax.nn.sigmoid(z))).astype(o_ref.dtype)


def s5_glu_t(y_t, p_t, d_skip, w_glu_t, b_glu, tm=256):
    width, tokens = y_t.shape
    tm = _tile(tokens, tm)
    col = lambda i: (0, i)
    return pl.pallas_call(
        _glu_t_kernel, grid=(tokens // tm,),
        in_specs=[pl.BlockSpec((width, tm), col), pl.BlockSpec((width, tm), col),
                  pl.BlockSpec((width, tm), lambda i: (1, i)),
                  pl.BlockSpec((width, 1), lambda i: (0, 0)),
                  pl.BlockSpec((width, width), lambda i: (0, 0)),
                  pl.BlockSpec((width, 1), lambda i: (0, 0))],
        out_specs=pl.BlockSpec((width, tm), col),
        out_shape=jax.ShapeDtypeStruct((width, tokens), BF16),
        compiler_params=_params("parallel"), name="s5_glu")(
            y_t, p_t, p_t, d_skip.reshape(width, 1), w_glu_t, b_glu.reshape(width, 1))


def _conv_kernel(hb_ref, gb_ref, gc_ref, zb_ref, hbp_ref, gcp_ref, w_ref, b_ref, o_ref, *, seq):
    tm = hb_ref.shape[0]
    v = gc_ref[...] * hb_ref[...]
    first = (pl.program_id(0) * tm) % seq == 0
    vp = jnp.where(first, 0.0, gcp_ref[...] * hbp_ref[...])
    ext = jnp.concatenate([vp, v], axis=0)
    conv = (w_ref[0:1, :] * ext[6:6 + tm] + w_ref[1:2, :] * ext[7:7 + tm] + w_ref[2:3, :] * v
            + b_ref[...])
    z = zb_ref[...]
    o_ref[...] = (gb_ref[...] * conv * (z * jax.nn.sigmoid(z))).astype(o_ref.dtype)


def short_conv_gate(pb, conv_w, conv_b, seq, tm=512, tc=512):
    tokens, w4 = pb.shape
    width = w4 // 4
    tm, tc = _tile(seq, tm), _tile(width, tc)
    ncb = width // tc
    seg = lambda s: pl.BlockSpec((tm, tc), lambda i, j: (i, s * ncb + j))
    prev = lambda s: pl.BlockSpec((8, tc), lambda i, j: (jnp.maximum(i * (tm // 8) - 1, 0), s * ncb + j))
    return pl.pallas_call(
        functools.partial(_conv_kernel, seq=seq), grid=(tokens // tm, ncb),
        in_specs=[seg(0), seg(1), seg(2), seg(3), prev(0), prev(2),
                  pl.BlockSpec((CONV_K, tc), lambda i, j: (0, j)),
                  pl.BlockSpec((1, tc), lambda i, j: (0, j))],
        out_specs=pl.BlockSpec((tm, tc), lambda i, j: (i, j)),
        out_shape=jax.ShapeDtypeStruct((tokens, width), BF16),
        compiler_params=_params("parallel", "parallel"), name="short_conv")(
            pb, pb, pb, pb, pb, pb, conv_w, conv_b.reshape(1, width))


def _out0_kernel(ya_ref, yb_ref, wa_ref, wb_ref, o_ref):
    acc = lax.dot_general(ya_ref[...], wa_ref[...], (((0,), (0,)), ((), ())), preferred_element_type=F32)
    o_ref[...] = acc + jnp.dot(yb_ref[...], wb_ref[...], preferred_element_type=F32)


def out_proj0(ya_t, yb, w_out, tm=1024, tn=1024):
    wa_rows, tokens = ya_t.shape
    wb_rows = yb.shape[1]
    n = w_out.shape[1]
    tm, tn = _tile(tokens, tm), _tile(n, tn)
    return pl.pallas_call(
        _out0_kernel, grid=(tokens // tm, n // tn),
        in_specs=[pl.BlockSpec((wa_rows, tm), lambda i, j: (0, i)),
                  pl.BlockSpec((tm, wb_rows), lambda i, j: (i, 0)),
                  pl.BlockSpec((wa_rows, tn), lambda i, j: (0, j)),
                  pl.BlockSpec((wb_rows, tn), lambda i, j: (wa_rows // wb_rows, j))],
        out_specs=pl.BlockSpec((tm, tn), lambda i, j: (i, j)),
        out_shape=jax.ShapeDtypeStruct((tokens, n), F32),
        compiler_params=_params("parallel", "parallel"), name="out_proj0")(ya_t, yb, w_out, w_out)


def _ln_kernel(h_ref, mix_ref, g_ref, b_ref, o_ref, ob_ref):
    y = DEEPNORM_ALPHA * h_ref[...] + mix_ref[...]
    mu = jnp.mean(y, axis=-1, keepdims=True)
    yc = y - mu
    var = jnp.mean(yc * yc, axis=-1, keepdims=True)
    out = yc * lax.rsqrt(var + LN_EPS) * g_ref[...] + b_ref[...]
    o_ref[...] = out
    ob_ref[...] = out.astype(BF16)


def deepnorm_ln(h, mix, g, b, tm=256):
    tokens, d = h.shape
    tm = _tile(tokens, tm)
    row = pl.BlockSpec((tm, d), lambda i: (i, 0))
    vec = pl.BlockSpec((1, d), lambda i: (0, 0))
    return pl.pallas_call(
        _ln_kernel, grid=(tokens // tm,),
        in_specs=[row, row, vec, vec], out_specs=[row, row],
        out_shape=[jax.ShapeDtypeStruct((tokens, d), F32), jax.ShapeDtypeStruct((tokens, d), BF16)],
        compiler_params=_params("parallel"), name="deepnorm_ln")(h, mix, g.reshape(1, d), b.reshape(1, d))


def _gate_kernel(hb_ref, w_ref, bg_ref, h_ref, p_ref, wp_ref, o_ref, ob_ref):
    gate = jax.nn.sigmoid(jnp.dot(hb_ref[...], w_ref[...], preferred_element_type=F32) + bg_ref[...])
    out = h_ref[...] + gate * jnp.dot(p_ref[...], wp_ref[...], preferred_element_type=F32)
    o_ref[...] = out
    ob_ref[...] = out.astype(BF16)


def ple_gate(h, hb, w_gate, b_gate, p, w_ple, tm=1024, tn=512):
    tokens, d = h.shape
    pd = p.shape[1]
    tm, tn = _tile(tokens, tm), _tile(d, tn)
    return pl.pallas_call(
        _gate_kernel, grid=(tokens // tm, d // tn),
        in_specs=[pl.BlockSpec((tm, d), lambda i, j: (i, 0)),
                  pl.BlockSpec((d, tn), lambda i, j: (0, j)),
                  pl.BlockSpec((1, tn), lambda i, j: (0, j)),
                  pl.BlockSpec((tm, tn), lambda i, j: (i, j)),
                  pl.BlockSpec((tm, pd), lambda i, j: (i, 0)),
                  pl.BlockSpec((pd, tn), lambda i, j: (0, j))],
        out_specs=[pl.BlockSpec((tm, tn), lambda i, j: (i, j))] * 2,
        out_shape=[jax.ShapeDtypeStruct((tokens, d), F32), jax.ShapeDtypeStruct((tokens, d), BF16)],
        compiler_params=_params("parallel", "parallel"), name="ple_gate")(
            hb, w_gate, b_gate.reshape(1, d), h, p, w_ple)


def _distance_tables(dil):
    groups = DEINT // dil
    rows = ATT_GROUP_ROWS[dil]
    f_q, a_q = np.divmod(np.arange(groups * rows), rows)
    iq = groups * a_q + f_q

    def table(k_rows, a0):
        f_k, a_k = np.divmod(np.arange(groups * k_rows), k_rows)
        delta = iq[:, None] - (groups * (a_k + a0) + f_k)[None, :]
        return np.where((delta >= 0) & (delta <= SPAN), dil * delta, FAR).astype(np.float32)

    return table(2 * rows, -rows), table(rows, 0)


def _attn_kernel(sl_ref, *refs):
    nb = len(DILATED)
    dist_refs, (q_ref, k_ref, v_ref, z_ref, o_ref, m_ref, l_ref, acc_ref) = refs[:2 * nb], refs[2 * nb:2 * nb + 8]
    bias_refs = refs[2 * nb + 8:]
    n_res = q_ref.shape[0] // DEINT
    slope = sl_ref[0]
    for d_ref, b_ref in zip(dist_refs, bias_refs):
        b_ref[...] = -slope * d_ref[...]

    def rows_of(ref, starts, size):
        return jnp.concatenate([ref[pl.ds(s, size), :] for s in starts], axis=0)

    def block_group(dil, c, items, tabs, fresh, final):
        groups, rows = DEINT // dil, ATT_GROUP_ROWS[dil]
        base = [(dil * f + c) * n_res for f in range(groups)]
        q_at = [[pl.multiple_of(b + rows * j, 16) for b in base] for j, _ in items]
        k_at = [qa if first else [pl.multiple_of(b + rows * (j - 1), 16) for b in base]
                for qa, (j, first) in zip(q_at, items)]
        k_rows = [rows if first else 2 * rows for _, first in items]
        bias = [tabs[1] if first else tabs[0] for _, first in items]
        n = range(len(items))
        ones = jnp.ones((2 * rows * groups, HEAD_DIM), BF16)
        q = [rows_of(q_ref, q_at[i], rows) for i in n]
        k = [rows_of(k_ref, k_at[i], k_rows[i]) for i in n]
        v1 = [jnp.concatenate([rows_of(v_ref, k_at[i], k_rows[i]), ones[:k_rows[i] * groups]], axis=1) for i in n]
        if not fresh:
            m_prev = [rows_of(m_ref, q_at[i], rows) for i in n]
            l_prev = [rows_of(l_ref, q_at[i], rows) for i in n]
            acc_prev = [rows_of(acc_ref, q_at[i], rows) for i in n]
        if final:
            z = [rows_of(z_ref, q_at[i], rows) for i in n]
        s = [lax.dot_general(q[i], k[i], (((1,), (1,)), ((), ())), preferred_element_type=F32)
             * (HEAD_DIM ** -0.5) + bias[i][...] for i in n]
        m_new = [jnp.broadcast_to(jnp.max(s[i], axis=-1, keepdims=True), (rows * groups, HEAD_DIM)) for i in n]
        if not fresh:
            m_new = [jnp.maximum(m_prev[i], m_new[i]) for i in n]
        p = [jnp.concatenate([jnp.exp(s[i][:, t:t + HEAD_DIM] - m_new[i])
                              for t in range(0, s[i].shape[1], HEAD_DIM)], axis=1).astype(BF16) for i in n]
        pv = [jnp.dot(p[i], v1[i], preferred_element_type=F32) for i in n]
        acc_new = [pv[i][:, :HEAD_DIM] for i in n]
        l_new = [pv[i][:, HEAD_DIM:] for i in n]
        if not fresh:
            alpha = [jnp.exp(m_prev[i] - m_new[i]) for i in n]
            l_new = [alpha[i] * l_prev[i] + l_new[i] for i in n]
            acc_new = [alpha[i] * acc_prev[i] + acc_new[i] for i in n]
        if final:
            out = [(acc_new[i] / l_new[i] * (z[i] * jax.nn.sigmoid(z[i]))).astype(o_ref.dtype) for i in n]
        for i in n:
            for f, at in enumerate(q_at[i]):
                part = slice(f * rows, (f + 1) * rows)
                if final:
                    o_ref[pl.ds(at, rows), :] = out[i][part]
                else:
                    m_ref[pl.ds(at, rows), :] = m_new[i][part]
                    l_ref[pl.ds(at, rows), :] = l_new[i][part]
                    acc_ref[pl.ds(at, rows), :] = acc_new[i][part]

    order = sorted(range(nb), key=lambda i: -DILATED[i][1])
    for pos, idx in enumerate(order):
        dil = DILATED[idx][1]
        fresh, final = pos == 0, pos == nb - 1
        tabs = (bias_refs[2 * idx], bias_refs[2 * idx + 1])
        n_blocks = n_res // ATT_GROUP_ROWS[dil]
        per = min(ATT_BLOCKS_PER_STEP[dil], n_blocks)
        assert n_blocks % per == 0

        def subsequence(c, carry, dil=dil, fresh=fresh, final=final, tabs=tabs, n_blocks=n_blocks, per=per):
            block_group(dil, c, [(j, j == 0) for j in range(per)], tabs, fresh, final)

            def later(g, carry2):
                block_group(dil, c, [(g * per + j, False) for j in range(per)], tabs, fresh, final)
                return carry2
            lax.fori_loop(1, n_blocks // per, later, 0)
            return carry
        lax.fori_loop(0, dil, subsequence, 0)


def dilated_attention(qkv, z, batch, seq):
    tokens, width = z.shape
    heads = width // HEAD_DIM
    slopes = jnp.asarray(np.exp2(-8.0 * np.arange(1, heads + 1) / heads).astype(np.float32)).reshape(heads, 1, 1)
    tables = [jnp.asarray(t) for _, dil in DILATED for t in _distance_tables(dil)]
    whole = lambda t: pl.BlockSpec(t.shape, lambda b, h: (0, 0))
    head = lambda off: pl.BlockSpec((seq, HEAD_DIM), lambda b, h: (b, off + h))
    return pl.pallas_call(
        _attn_kernel, grid=(batch, heads),
        in_specs=[pl.BlockSpec((1, 1, 1), lambda b, h: (h, 0, 0))] + [whole(t) for t in tables]
        + [head(0), head(heads), head(2 * heads), head(0)],
        out_specs=head(0),
        out_shape=jax.ShapeDtypeStruct((tokens, width), BF16),
        scratch_shapes=[pltpu.VMEM((seq, HEAD_DIM), F32)] * 3
        + [pltpu.VMEM(t.shape, F32) for t in tables],
        compiler_params=_params("parallel", "parallel"),
        name="dilated_attn")(slopes, *tables, qkv, qkv, qkv, z)


def kernel(x, p, ab_w_in, ab_lambda_re, ab_lambda_im, ab_log_step, ab_b_re, ab_b_im, ab_c_re, ab_c_im, ab_d,
           ab_w_glu, ab_b_glu, ab_conv_w, ab_conv_b, ab_w_out, at_w_in, at_w_out, ln_g, ln_b, ple_w,
           ple_gate_w, ple_gate_b):
    batch, seq, d = x.shape
    tokens = batch * seq
    s5w = d // 2
    assert seq % (DEINT * SPAN) == 0 and seq % CHUNK == 0 and s5w % S5_GROUP == 0
    h = x.reshape(tokens, d)
    hb = h.astype(BF16)
    pb = p.reshape(DEPTH, tokens, PLE_DIM).astype(BF16)

    w_in = ab_w_in[0].astype(BF16)
    proj_t = matmul_nt(w_in[:, :2 * s5w].T, hb, F32)
    proj_b = matmul_nn(hb, w_in[:, 2 * s5w:], F32)
    y_t = s5_core(proj_t[:s5w], ab_log_step[0], ab_lambda_re[0], ab_lambda_im[0], ab_b_re[0], ab_b_im[0],
                  ab_c_re[0], ab_c_im[0], batch)
    ya_t = s5_glu_t(y_t, proj_t, ab_d[0], ab_w_glu[0].T.astype(BF16), ab_b_glu[0])
    yb = short_conv_gate(proj_b, ab_conv_w[0], ab_conv_b[0], seq)
    mix = out_proj0(ya_t, yb, ab_w_out[0].astype(BF16))
    h, hb = deepnorm_ln(h, mix, ln_g[0], ln_b[0])
    h, hb = ple_gate(h, hb, ple_gate_w[0].astype(BF16), ple_gate_b[0], pb[0], ple_w[0].astype(BF16))

    w_in = at_w_in[0].astype(BF16)
    qkv = matmul_nn(hb, w_in[:, :3 * d], BF16, seq=seq, deint="lhs")
    z = matmul_nn(hb, w_in[:, 3 * d:], F32, seq=seq, deint="lhs")
    og = dilated_attention(qkv, z, batch, seq)
    mix = matmul_nn(og, at_w_out[0].astype(BF16), F32, seq=seq, deint="out")
    h, hb = deepnorm_ln(h, mix, ln_g[1], ln_b[1])
    h, _ = ple_gate(h, hb, ple_gate_w[1].astype(BF16), ple_gate_b[1], pb[1], ple_w[1].astype(BF16))
    return h.reshape(batch, seq, d)
```

```python
import functools
import math

import jax
import jax.numpy as jnp
import numpy as np
from jax import lax
from jax.experimental import pallas as pl
from jax.experimental.pallas import tpu as pltpu

F32 = jnp.float32
BF16 = jnp.bfloat16

V7X_VMEM_LIMIT_BYTES = 56 * 1024 * 1024
LANES = 128
S5_GROUP = 16
S5_STATE = 64
CHUNK = LANES
CONV_K = 3
HEAD_DIM = 128
PLE_DIM = 256
LN_EPS = 1e-5
DEPTH = 2
DEEPNORM_ALPHA = (2.0 * DEPTH) ** 0.25
DILATED = ((128, 1), (512, 4), (2048, 16))
SPAN = 128
DEINT = max(d for _, d in DILATED)
ATT_GROUP_ROWS = {16: 128, 4: 32, 1: 8}
FAR = 1e32
ATT_BLOCKS_PER_STEP = {16: 4, 4: 4, 1: 4}
assert all(w // d == SPAN for w, d in DILATED)


def _params(*sem):
    return pltpu.CompilerParams(dimension_semantics=sem, vmem_limit_bytes=V7X_VMEM_LIMIT_BYTES)


def _tile(n, pref):
    return pref if n % pref == 0 else n


def _mm_nn_kernel(a_ref, b_ref, o_ref):
    o_ref[...] = jnp.dot(a_ref[...], b_ref[...], preferred_element_type=F32).astype(o_ref.dtype)


def _mm_nt_kernel(a_ref, b_ref, o_ref):
    o_ref[...] = lax.dot_general(a_ref[...], b_ref[...], (((1,), (1,)), ((), ())),
                                 preferred_element_type=F32).astype(o_ref.dtype)


def matmul_nn(a, b, out_dtype, tm=1024, tn=1024):
    m, k = a.shape
    _, n = b.shape
    tm, tn = _tile(m, tm), _tile(n, tn)
    return pl.pallas_call(
        _mm_nn_kernel, grid=(m // tm, n // tn),
        in_specs=[pl.BlockSpec((tm, k), lambda i, j: (i, 0)),
                  pl.BlockSpec((k, tn), lambda i, j: (0, j))],
        out_specs=pl.BlockSpec((tm, tn), lambda i, j: (i, j)),
        out_shape=jax.ShapeDtypeStruct((m, n), out_dtype),
        compiler_params=_params("parallel", "parallel"), name="mm_nn")(a, b)


def _mm_deint_kernel(perm_ref, a_ref, b_ref, o_ref, lhs_ref):
    @pl.when(pl.program_id(1) == 0)
    def _():
        lhs_ref[...] = jnp.dot(perm_ref[...], a_ref[...], preferred_element_type=F32).astype(lhs_ref.dtype)

    acc = jnp.dot(lhs_ref[...], b_ref[...], preferred_element_type=F32)
    rows = o_ref.shape[1]
    for r in range(DEINT):
        o_ref[r] = acc[r * rows:(r + 1) * rows].astype(o_ref.dtype)


def matmul_deinterleaved(a, b, out_dtype, seq, tm=512, tn=1024):
    m, k = a.shape
    _, n = b.shape
    tm, tn = _tile(seq, tm), _tile(n, tn)
    rows, nt = tm // DEINT, seq // tm
    perm = np.zeros((tm, tm), np.float32)
    tok = np.arange(tm)
    perm[(tok % DEINT) * rows + tok // DEINT, tok] = 1.0
    return pl.pallas_call(
        _mm_deint_kernel, grid=(m // tm, n // tn),
        in_specs=[pl.BlockSpec((tm, tm), lambda i, j: (0, 0)),
                  pl.BlockSpec((tm, k), lambda i, j: (i, 0)),
                  pl.BlockSpec((k, tn), lambda i, j: (0, j))],
        out_specs=pl.BlockSpec((DEINT, rows, tn), lambda i, j: (i // nt, i % nt, j)),
        out_shape=jax.ShapeDtypeStruct((m // seq * DEINT, seq // DEINT, n), out_dtype),
        scratch_shapes=[pltpu.VMEM((tm, k), a.dtype)],
        compiler_params=_params("parallel", "arbitrary"), name="mm_deint")(jnp.asarray(perm, a.dtype), a, b)


def matmul_nt(a, b, out_dtype, tm=1024, tn=1024):
    m, k = a.shape
    n, _ = b.shape
    tm, tn = _tile(m, tm), _tile(n, tn)
    return pl.pallas_call(
        _mm_nt_kernel, grid=(m // tm, n // tn),
        in_specs=[pl.BlockSpec((tm, k), lambda i, j: (i, 0)),
                  pl.BlockSpec((tn, k), lambda i, j: (j, 0))],
        out_specs=pl.BlockSpec((tm, tn), lambda i, j: (i, j)),
        out_shape=jax.ShapeDtypeStruct((m, n), out_dtype),
        compiler_params=_params("parallel", "parallel"), name="mm_nt")(a, b)


def _s5_kernel(ls_ref, laml_ref, lams_ref, bt_ref, c_ref, ct_ref, u_ref, o_ref,
               kmat_ref, m_ref, wst_re_ref, wst_im_ref, wa_ref, wb_ref,
               sre_ref, sim_ref, hre_ref, him_ref, *, batch):
    grp = S5_GROUP
    nc = u_ref.shape[1]
    nck = nc // batch
    dt = jnp.exp(ls_ref[0])
    a_re_l = jnp.minimum(laml_ref[0, 0:1, :], -1e-4)
    a_im_l = laml_ref[0, 1:2, :]
    a_re_s = jnp.minimum(lams_ref[0, :, 0:1], -1e-4)
    a_im_s = lams_ref[0, :, 1:2]

    def cpow(a_re, a_im, n):
        mag = jnp.exp(n * (dt * a_re))
        ang = n * (dt * a_im)
        return mag * jnp.cos(ang), mag * jnp.sin(ang)

    ab_re, ab_im = cpow(a_re_l, a_im_l, 1.0)
    den = a_re_l * a_re_l + a_im_l * a_im_l
    nr = ab_re - 1.0
    co_re = (nr * a_re_l + ab_im * a_im_l) / den
    co_im = (ab_im * a_re_l - nr * a_im_l) / den
    br_t, bi_t = bt_ref[0, 0], bt_ref[0, 1]
    bbr = co_re * br_t - co_im * bi_t
    bbi = co_re * bi_t + co_im * br_t
    cr, ci_ = c_ref[0, 0], c_ref[0, 1]

    g1 = jnp.concatenate([bbr[i:i + 1] * cr - bbi[i:i + 1] * ci_ for i in range(grp)], axis=0)
    g2 = jnp.concatenate([bbr[i:i + 1] * ci_ + bbi[i:i + 1] * cr for i in range(grp)], axis=0)
    tau = lax.broadcasted_iota(jnp.int32, (S5_STATE, CHUNK), 1).astype(F32)
    e_re, e_im = cpow(a_re_s, a_im_s, tau)
    kmat_ref[...] = (jnp.dot(g1, e_re, precision=lax.Precision.HIGHEST, preferred_element_type=F32)
                     - jnp.dot(g2, e_im, precision=lax.Precision.HIGHEST, preferred_element_type=F32))

    causal = (lax.broadcasted_iota(jnp.int32, (CHUNK, CHUNK), 1)
              >= lax.broadcasted_iota(jnp.int32, (CHUNK, CHUNK), 0))

    def toeplitz_rows(ci, carry):
        row0 = pl.multiple_of(ci * CHUNK, CHUNK)
        for co in range(grp):
            krow = kmat_ref[pl.ds(ci * grp + co, 1), :]
            shifted = pltpu.roll(jnp.broadcast_to(krow, (CHUNK, CHUNK)), 0, 1, stride=1, stride_axis=0)
            m_ref[pl.ds(row0, CHUNK), co * CHUNK:(co + 1) * CHUNK] = (
                jnp.where(causal, shifted, 0.0).astype(BF16))
        return carry

    lax.fori_loop(0, grp, toeplitz_rows, 0)

    rev = (CHUNK - 1 - lax.broadcasted_iota(jnp.int32, (CHUNK, S5_STATE), 0)).astype(F32)
    er_re, er_im = cpow(a_re_l, a_im_l, rev)
    for i in range(grp):
        wst_re_ref[i * CHUNK:(i + 1) * CHUNK, :] = (er_re * bbr[i:i + 1] - er_im * bbi[i:i + 1]).astype(BF16)
        wst_im_ref[i * CHUNK:(i + 1) * CHUNK, :] = (er_re * bbi[i:i + 1] + er_im * bbr[i:i + 1]).astype(BF16)

    t1 = (lax.broadcasted_iota(jnp.int32, (S5_STATE, CHUNK), 1) + 1).astype(F32)
    e1_re, e1_im = cpow(a_re_s, a_im_s, t1)
    cr_t, ci_t = ct_ref[0, 0], ct_ref[0, 1]
    for o in range(grp):
        wa_ref[:, o * CHUNK:(o + 1) * CHUNK] = (cr_t[:, o:o + 1] * e1_re - ci_t[:, o:o + 1] * e1_im).astype(BF16)
        wb_ref[:, o * CHUNK:(o + 1) * CHUNK] = (-(cr_t[:, o:o + 1] * e1_im + ci_t[:, o:o + 1] * e1_re)).astype(BF16)

    lhs = jnp.concatenate([u_ref[i].astype(BF16) for i in range(grp)], axis=1)
    sre_ref[...] = jnp.dot(lhs, wst_re_ref[...], preferred_element_type=F32)
    sim_ref[...] = jnp.dot(lhs, wst_im_ref[...], preferred_element_type=F32)

    ac_re, ac_im = cpow(a_re_l, a_im_l, float(CHUNK))

    def carry_state(k, carry):
        new = []
        for b in range(batch):
            h_re, h_im = carry[2 * b], carry[2 * b + 1]
            row = b * nck + k
            hre_ref[pl.ds(row, 1), :] = h_re
            him_ref[pl.ds(row, 1), :] = h_im
            new.append(ac_re * h_re - ac_im * h_im + sre_ref[pl.ds(row, 1), :])
            new.append(ac_re * h_im + ac_im * h_re + sim_ref[pl.ds(row, 1), :])
        return tuple(new)

    zero = jnp.zeros((1, S5_STATE), F32)
    lax.fori_loop(0, nck, carry_state, (zero,) * (2 * batch))

    y = (jnp.dot(lhs, m_ref[...], preferred_element_type=F32)
         + jnp.dot(hre_ref[...].astype(BF16), wa_ref[...], preferred_element_type=F32)
         + jnp.dot(him_ref[...].astype(BF16), wb_ref[...], preferred_element_type=F32))
    for o in range(grp):
        o_ref[o] = y[:, o * CHUNK:(o + 1) * CHUNK]


def s5_core(u_t, log_step, lam_re, lam_im, b_re, b_im, c_re, c_im, batch):
    width, tokens = u_t.shape
    groups = width // S5_GROUP
    nc = tokens // CHUNK
    u3 = u_t.reshape(width, nc, CHUNK)
    ls = log_step.reshape(groups, 1, 1)
    lam_l = jnp.stack([lam_re, lam_im], axis=1)
    lam_s = jnp.stack([lam_re, lam_im], axis=2)
    b_t = jnp.stack([b_re, b_im], axis=1).transpose(0, 1, 3, 2)
    c = jnp.stack([c_re, c_im], axis=1)
    c_t = c.transpose(0, 1, 3, 2)
    gw = S5_GROUP * CHUNK
    small = lambda *shape: pl.BlockSpec((1,) + shape, lambda g: (g,) + (0,) * len(shape))
    out = pl.pallas_call(
        functools.partial(_s5_kernel, batch=batch), grid=(groups,),
        in_specs=[small(1, 1), small(2, S5_STATE), small(S5_STATE, 2),
                  small(2, S5_GROUP, S5_STATE), small(2, S5_GROUP, S5_STATE), small(2, S5_STATE, S5_GROUP),
                  pl.BlockSpec((S5_GROUP, nc, CHUNK), lambda g: (g, 0, 0))],
        out_specs=pl.BlockSpec((S5_GROUP, nc, CHUNK), lambda g: (g, 0, 0)),
        out_shape=jax.ShapeDtypeStruct((width, nc, CHUNK), F32),
        scratch_shapes=[pltpu.VMEM((S5_GROUP * S5_GROUP, CHUNK), F32),
                        pltpu.VMEM((gw, gw), BF16),
                        pltpu.VMEM((gw, S5_STATE), BF16), pltpu.VMEM((gw, S5_STATE), BF16),
                        pltpu.VMEM((S5_STATE, gw), BF16), pltpu.VMEM((S5_STATE, gw), BF16),
                        pltpu.VMEM((nc, S5_STATE), F32), pltpu.VMEM((nc, S5_STATE), F32),
                        pltpu.VMEM((nc, S5_STATE), F32), pltpu.VMEM((nc, S5_STATE), F32)],
        compiler_params=_params("parallel"), name="s5_core")(ls, lam_l, lam_s, b_t, c, c_t, u3)
    return out.reshape(width, tokens)


def _glu_t_kernel(y_ref, u_ref, z_ref, d_ref, w_ref, b_ref, o_ref):
    y = jax.nn.gelu(y_ref[...] + d_ref[...] * u_ref[...])
    acc = jnp.dot(w_ref[...], y.astype(BF16), preferred_element_type=F32)
    z = z_ref[...]
    o_ref[...] = (y * jax.nn.sigmoid(acc + b_ref[...]) * (z * jax.nn.sigmoid(z))).astype(o_ref.dtype)


def s5_glu_t(y_t, p_t, d_skip, w_glu_t, b_glu, tm=256):
    width, tokens = y_t.shape
    tm = _tile(tokens, tm)
    col = lambda i: (0, i)
    return pl.pallas_call(
        _glu_t_kernel, grid=(tokens // tm,),
        in_specs=[pl.BlockSpec((width, tm), col), pl.BlockSpec((width, tm), col),
                  pl.BlockSpec((width, tm), lambda i: (1, i)),
                  pl.BlockSpec((width, 1), lambda i: (0, 0)),
                  pl.BlockSpec((width, width), lambda i: (0, 0)),
                  pl.BlockSpec((width, 1), lambda i: (0, 0))],
        out_specs=pl.BlockSpec((width, tm), col),
        out_shape=jax.ShapeDtypeStruct((width, tokens), BF16),
        compiler_params=_params("parallel"), name="s5_glu")(
            y_t, p_t, p_t, d_skip.reshape(width, 1), w_glu_t, b_glu.reshape(width, 1))


def _conv_kernel(hb_ref, gb_ref, gc_ref, zb_ref, hbp_ref, gcp_ref, w_ref, b_ref, o_ref, *, seq):
    tm = hb_ref.shape[0]
    v = gc_ref[...] * hb_ref[...]
    first = (pl.program_id(0) * tm) % seq == 0
    vp = jnp.where(first, 0.0, gcp_ref[...] * hbp_ref[...])
    ext = jnp.concatenate([vp, v], axis=0)
    conv = (w_ref[0:1, :] * ext[6:6 + tm] + w_ref[1:2, :] * ext[7:7 + tm] + w_ref[2:3, :] * v
            + b_ref[...])
    z = zb_ref[...]
    o_ref[...] = (gb_ref[...] * conv * (z * jax.nn.sigmoid(z))).astype(o_ref.dtype)


def short_conv_gate(pb, conv_w, conv_b, seq, tm=512, tc=512):
    tokens, w4 = pb.shape
    width = w4 // 4
    tm, tc = _tile(seq, tm), _tile(width, tc)
    ncb = width // tc
    seg = lambda s: pl.BlockSpec((tm, tc), lambda i, j: (i, s * ncb + j))
    prev = lambda s: pl.BlockSpec((8, tc), lambda i, j: (jnp.maximum(i * (tm // 8) - 1, 0), s * ncb + j))
    return pl.pallas_call(
        functools.partial(_conv_kernel, seq=seq), grid=(tokens // tm, ncb),
        in_specs=[seg(0), seg(1), seg(2), seg(3), prev(0), prev(2),
                  pl.BlockSpec((CONV_K, tc), lambda i, j: (0, j)),
                  pl.BlockSpec((1, tc), lambda i, j: (0, j))],
        out_specs=pl.BlockSpec((tm, tc), lambda i, j: (i, j)),
        out_shape=jax.ShapeDtypeStruct((tokens, width), BF16),
        compiler_params=_params("parallel", "parallel"), name="short_conv")(
            pb, pb, pb, pb, pb, pb, conv_w, conv_b.reshape(1, width))


def _out0_kernel(ya_ref, yb_ref, wa_ref, wb_ref, o_ref):
    acc = lax.dot_general(ya_ref[...], wa_ref[...], (((0,), (0,)), ((), ())), preferred_element_type=F32)
    o_ref[...] = acc + jnp.dot(yb_ref[...], wb_ref[...], preferred_element_type=F32)


def out_proj0(ya_t, yb, w_out, tm=1024, tn=1024):
    wa_rows, tokens = ya_t.shape
    wb_rows = yb.shape[1]
    n = w_out.shape[1]
    tm, tn = _tile(tokens, tm), _tile(n, tn)
    return pl.pallas_call(
        _out0_kernel, grid=(tokens // tm, n // tn),
        in_specs=[pl.BlockSpec((wa_rows, tm), lambda i, j: (0, i)),
                  pl.BlockSpec((tm, wb_rows), lambda i, j: (i, 0)),
                  pl.BlockSpec((wa_rows, tn), lambda i, j: (0, j)),
                  pl.BlockSpec((wb_rows, tn), lambda i, j: (wa_rows // wb_rows, j))],
        out_specs=pl.BlockSpec((tm, tn), lambda i, j: (i, j)),
        out_shape=jax.ShapeDtypeStruct((tokens, n), F32),
        compiler_params=_params("parallel", "parallel"), name="out_proj0")(ya_t, yb, w_out, w_out)


def _ln_kernel(h_ref, mix_ref, g_ref, b_ref, o_ref, ob_ref):
    y = DEEPNORM_ALPHA * h_ref[...] + mix_ref[...]
    mu = jnp.mean(y, axis=-1, keepdims=True)
    yc = y - mu
    var = jnp.mean(yc * yc, axis=-1, keepdims=True)
    out = yc * lax.rsqrt(var + LN_EPS) * g_ref[...] + b_ref[...]
    o_ref[...] = out
    ob_ref[...] = out.astype(BF16)


def deepnorm_ln(h, mix, g, b, tm=256):
    tokens, d = h.shape
    tm = _tile(tokens, tm)
    row = pl.BlockSpec((tm, d), lambda i: (i, 0))
    vec = pl.BlockSpec((1, d), lambda i: (0, 0))
    return pl.pallas_call(
        _ln_kernel, grid=(tokens // tm,),
        in_specs=[row, row, vec, vec], out_specs=[row, row],
        out_shape=[jax.ShapeDtypeStruct((tokens, d), F32), jax.ShapeDtypeStruct((tokens, d), BF16)],
        compiler_params=_params("parallel"), name="deepnorm_ln")(h, mix, g.reshape(1, d), b.reshape(1, d))


def _gate_kernel(hb_ref, w_ref, bg_ref, h_ref, p_ref, wp_ref, o_ref, ob_ref):
    gate = jax.nn.sigmoid(jnp.dot(hb_ref[...], w_ref[...], preferred_element_type=F32) + bg_ref[...])
    out = h_ref[...] + gate * jnp.dot(p_ref[...], wp_ref[...], preferred_element_type=F32)
    o_ref[...] = out
    ob_ref[...] = out.astype(BF16)


def ple_gate(h, hb, w_gate, b_gate, p, w_ple, tm=1024, tn=512):
    tokens, d = h.shape
    pd = p.shape[1]
    tm, tn = _tile(tokens, tm), _tile(d, tn)
    return pl.pallas_call(
        _gate_kernel, grid=(tokens // tm, d // tn),
        in_specs=[pl.BlockSpec((tm, d), lambda i, j: (i, 0)),
                  pl.BlockSpec((d, tn), lambda i, j: (0, j)),
                  pl.BlockSpec((1, tn), lambda i, j: (0, j)),
                  pl.BlockSpec((tm, tn), lambda i, j: (i, j)),
                  pl.BlockSpec((tm, pd), lambda i, j: (i, 0)),
                  pl.BlockSpec((pd, tn), lambda i, j: (0, j))],
        out_specs=[pl.BlockSpec((tm, tn), lambda i, j: (i, j))] * 2,
        out_shape=[jax.ShapeDtypeStruct((tokens, d), F32), jax.ShapeDtypeStruct((tokens, d), BF16)],
        compiler_params=_params("parallel", "parallel"), name="ple_gate")(
            hb, w_gate, b_gate.reshape(1, d), h, p, w_ple)


def _distance_tables(dil):
    groups = DEINT // dil
    rows = ATT_GROUP_ROWS[dil]
    f_q, a_q = np.divmod(np.arange(groups * rows), rows)
    iq = groups * a_q + f_q

    def table(k_rows, a0):
        f_k, a_k = np.divmod(np.arange(groups * k_rows), k_rows)
        delta = iq[:, None] - (groups * (a_k + a0) + f_k)[None, :]
        return np.where((delta >= 0) & (delta <= SPAN), dil * delta, FAR).astype(np.float32)

    return table(2 * rows, -rows), table(rows, 0)


def _attn_kernel(sl_ref, *refs):
    nb = len(DILATED)
    dist_refs, unperm_ref = refs[:2 * nb], refs[2 * nb]
    q_ref, k_ref, v_ref, z_ref, o_ref, m_ref, l_ref, acc_ref = refs[2 * nb + 1:2 * nb + 9]
    bias_refs = refs[2 * nb + 9:]
    n_res = q_ref.shape[1]
    slope = sl_ref[0]
    for d_ref, b_ref in zip(dist_refs, bias_refs):
        b_ref[...] = -slope * d_ref[...]

    def rows_of(ref, res, starts, size):
        if len(ref.shape) == 3:
            return jnp.concatenate([ref[r, pl.ds(s, size), :] for r, s in zip(res, starts)], axis=0)
        return jnp.concatenate([ref[pl.ds(r * n_res + s, size), :] for r, s in zip(res, starts)], axis=0)

    def block_group(dil, c, items, tabs, fresh, final):
        groups, rows = DEINT // dil, ATT_GROUP_ROWS[dil]
        align = min(rows, 16)
        res = [dil * f + c for f in range(groups)]
        q_at = [[pl.multiple_of(rows * j, align)] * groups for j, _ in items]
        k_at = [qa if first else [pl.multiple_of(rows * (j - 1), align)] * groups
                for qa, (j, first) in zip(q_at, items)]
        k_rows = [rows if first else 2 * rows for _, first in items]
        bias = [tabs[1] if first else tabs[0] for _, first in items]
        n = range(len(items))
        ones = jnp.ones((2 * rows * groups, HEAD_DIM), BF16)
        q = [rows_of(q_ref, res, q_at[i], rows) for i in n]
        k = [rows_of(k_ref, res, k_at[i], k_rows[i]) for i in n]
        v1 = [jnp.concatenate([rows_of(v_ref, res, k_at[i], k_rows[i]), ones[:k_rows[i] * groups]], axis=1)
              for i in n]
        if not fresh:
            m_prev = [rows_of(m_ref, res, q_at[i], rows) for i in n]
            l_prev = [rows_of(l_ref, res, q_at[i], rows) for i in n]
            acc_prev = [rows_of(acc_ref, res, q_at[i], rows) for i in n]
        s = [lax.dot_general(q[i], k[i], (((1,), (1,)), ((), ())), preferred_element_type=F32)
             * (HEAD_DIM ** -0.5) + bias[i][...] for i in n]
        m_new = [jnp.broadcast_to(jnp.max(s[i], axis=-1, keepdims=True), (rows * groups, HEAD_DIM)) for i in n]
        if not fresh:
            m_new = [jnp.maximum(m_prev[i], m_new[i]) for i in n]
        p = [jnp.concatenate([jnp.exp(s[i][:, t:t + HEAD_DIM] - m_new[i])
                              for t in range(0, s[i].shape[1], HEAD_DIM)], axis=1).astype(BF16) for i in n]
        pv = [jnp.dot(p[i], v1[i], preferred_element_type=F32) for i in n]
        acc_new = [pv[i][:, :HEAD_DIM] for i in n]
        l_new = [pv[i][:, HEAD_DIM:] for i in n]
        if not fresh:
            alpha = [jnp.exp(m_prev[i] - m_new[i]) for i in n]
            l_new = [alpha[i] * l_prev[i] + l_new[i] for i in n]
            acc_new = [alpha[i] * acc_prev[i] + acc_new[i] for i in n]
        for i in n:
            if final:
                span = rows * groups
                at = pl.multiple_of(span * items[i][0], span)
                o = jnp.dot(unperm_ref[...], (acc_new[i] / l_new[i]).astype(BF16), preferred_element_type=F32)
                z = z_ref[pl.ds(at, span), :]
                o_ref[pl.ds(at, span), :] = (o * (z * jax.nn.sigmoid(z))).astype(o_ref.dtype)
                continue
            for f in range(groups):
                at = res[f] * n_res + q_at[i][f]
                part = slice(f * rows, (f + 1) * rows)
                m_ref[pl.ds(at, rows), :] = m_new[i][part]
                l_ref[pl.ds(at, rows), :] = l_new[i][part]
                acc_ref[pl.ds(at, rows), :] = acc_new[i][part]

    order = sorted(range(nb), key=lambda i: -DILATED[i][1])
    for pos, idx in enumerate(order):
        dil = DILATED[idx][1]
        fresh, final = pos == 0, pos == nb - 1
        tabs = (bias_refs[2 * idx], bias_refs[2 * idx + 1])
        n_blocks = n_res // ATT_GROUP_ROWS[dil]
        per = min(ATT_BLOCKS_PER_STEP[dil], n_blocks)
        assert n_blocks % per == 0

        def subsequence(c, carry, dil=dil, fresh=fresh, final=final, tabs=tabs, n_blocks=n_blocks, per=per):
            block_group(dil, c, [(j, j == 0) for j in range(per)], tabs, fresh, final)

            def later(g, carry2):
                block_group(dil, c, [(g * per + j, False) for j in range(per)], tabs, fresh, final)
                return carry2
            lax.fori_loop(1, n_blocks // per, later, 0)
            return carry
        lax.fori_loop(0, dil, subsequence, 0)


def dilated_attention(qkv, z, batch, seq):
    tokens, width = z.shape
    heads = width // HEAD_DIM
    n_res = seq // DEINT
    slopes = jnp.asarray(np.exp2(-8.0 * np.arange(1, heads + 1) / heads).astype(np.float32)).reshape(heads, 1, 1)
    tables = [jnp.asarray(t) for _, dil in DILATED for t in _distance_tables(dil)]
    assert min(d for _, d in DILATED) == 1
    rows = ATT_GROUP_ROWS[1]
    tok = np.arange(DEINT * rows)
    unperm = np.zeros((DEINT * rows, DEINT * rows), np.float32)
    unperm[tok, (tok % DEINT) * rows + tok // DEINT] = 1.0
    whole = lambda t: pl.BlockSpec(t.shape, lambda b, h: (0, 0))
    grouped = lambda off: pl.BlockSpec((DEINT, n_res, HEAD_DIM), lambda b, h: (b, 0, off + h))
    natural = pl.BlockSpec((seq, HEAD_DIM), lambda b, h: (b, h))
    return pl.pallas_call(
        _attn_kernel, grid=(batch, heads),
        in_specs=[pl.BlockSpec((1, 1, 1), lambda b, h: (h, 0, 0))] + [whole(t) for t in tables]
        + [whole(unperm), grouped(0), grouped(heads), grouped(2 * heads), natural],
        out_specs=natural,
        out_shape=jax.ShapeDtypeStruct((tokens, width), BF16),
        scratch_shapes=[pltpu.VMEM((seq, HEAD_DIM), F32)] * 3
        + [pltpu.VMEM(t.shape, F32) for t in tables],
        compiler_params=_params("parallel", "parallel"),
        name="dilated_attn")(slopes, *tables, jnp.asarray(unperm, BF16), qkv, qkv, qkv, z)


def kernel(x, p, ab_w_in, ab_lambda_re, ab_lambda_im, ab_log_step, ab_b_re, ab_b_im, ab_c_re, ab_c_im, ab_d,
           ab_w_glu, ab_b_glu, ab_conv_w, ab_conv_b, ab_w_out, at_w_in, at_w_out, ln_g, ln_b, ple_w,
           ple_gate_w, ple_gate_b):
    batch, seq, d = x.shape
    tokens = batch * seq
    s5w = d // 2
    assert seq % (DEINT * SPAN) == 0 and seq % CHUNK == 0 and s5w % S5_GROUP == 0
    h = x.reshape(tokens, d)
    hb = h.astype(BF16)
    pb = p.reshape(DEPTH, tokens, PLE_DIM).astype(BF16)

    w_in = ab_w_in[0].astype(BF16)
    proj_t = matmul_nt(w_in[:, :2 * s5w].T, hb, F32)
    proj_b = matmul_nn(hb, w_in[:, 2 * s5w:], F32)
    y_t = s5_core(proj_t[:s5w], ab_log_step[0], ab_lambda_re[0], ab_lambda_im[0], ab_b_re[0], ab_b_im[0],
                  ab_c_re[0], ab_c_im[0], batch)
    ya_t = s5_glu_t(y_t, proj_t, ab_d[0], ab_w_glu[0].T.astype(BF16), ab_b_glu[0])
    yb = short_conv_gate(proj_b, ab_conv_w[0], ab_conv_b[0], seq)
    mix = out_proj0(ya_t, yb, ab_w_out[0].astype(BF16))
    h, hb = deepnorm_ln(h, mix, ln_g[0], ln_b[0])
    h, hb = ple_gate(h, hb, ple_gate_w[0].astype(BF16), ple_gate_b[0], pb[0], ple_w[0].astype(BF16))

    w_in = at_w_in[0].astype(BF16)
    qkv = matmul_deinterleaved(hb, w_in[:, :3 * d], BF16, seq)
    z = matmul_nn(hb, w_in[:, 3 * d:], F32)
    og = dilated_attention(qkv, z, batch, seq)
    mix = matmul_nn(og, at_w_out[0].astype(BF16), F32)
    h, hb = deepnorm_ln(h, mix, ln_g[1], ln_b[1])
    h, _ = ple_gate(h, hb, ple_gate_w[1].astype(BF16), ple_gate_b[1], pb[1], ple_w[1].astype(BF16))
    return h.reshape(batch, seq, d)
```

```python
import functools
import math

import jax
import jax.numpy as jnp
import numpy as np
from jax import lax
from jax.experimental import pallas as pl
from jax.experimental.pallas import tpu as pltpu

F32 = jnp.float32
BF16 = jnp.bfloat16

V7X_VMEM_LIMIT_BYTES = 56 * 1024 * 1024
LANES = 128
S5_GROUP = 16
S5_STATE = 64
CHUNK = LANES
CONV_K = 3
HEAD_DIM = 128
PLE_DIM = 256
LN_EPS = 1e-5
DEPTH = 2
DEEPNORM_ALPHA = (2.0 * DEPTH) ** 0.25
DILATED = ((128, 1), (512, 4), (2048, 16))
SPAN = 128
DEINT = max(d for _, d in DILATED)
ATT_GROUP_ROWS = {16: 128, 4: 32, 1: 8}
FAR = 1e32
ATT_BLOCKS_PER_STEP = {16: 4, 4: 4, 1: 4}
assert all(w // d == SPAN for w, d in DILATED)


def _params(*sem):
    return pltpu.CompilerParams(dimension_semantics=sem, vmem_limit_bytes=V7X_VMEM_LIMIT_BYTES)


def _tile(n, pref):
    return pref if n % pref == 0 else n


def _mm_nn_kernel(a_ref, b_ref, o_ref):
    o_ref[...] = jnp.dot(a_ref[...], b_ref[...], preferred_element_type=F32).astype(o_ref.dtype)


def _mm_nt_kernel(a_ref, b_ref, o_ref):
    o_ref[...] = lax.dot_general(a_ref[...], b_ref[...], (((1,), (1,)), ((), ())),
                                 preferred_element_type=F32).astype(o_ref.dtype)


def matmul_nn(a, b, out_dtype, tm=1024, tn=1024):
    m, k = a.shape
    _, n = b.shape
    tm, tn = _tile(m, tm), _tile(n, tn)
    return pl.pallas_call(
        _mm_nn_kernel, grid=(m // tm, n // tn),
        in_specs=[pl.BlockSpec((tm, k), lambda i, j: (i, 0)),
                  pl.BlockSpec((k, tn), lambda i, j: (0, j))],
        out_specs=pl.BlockSpec((tm, tn), lambda i, j: (i, j)),
        out_shape=jax.ShapeDtypeStruct((m, n), out_dtype),
        compiler_params=_params("parallel", "parallel"), name="mm_nn")(a, b)


DEINT_SPAN = 512


def _mm_deint_kernel(perm_ref, a_ref, b_ref, o_ref, lhs_ref):
    span = perm_ref.shape[0]
    runs = a_ref.shape[0] // span
    rows = span // DEINT

    @pl.when(pl.program_id(1) == 0)
    def _():
        for t in range(runs):
            lhs_ref[t * span:(t + 1) * span, :] = jnp.dot(
                perm_ref[...], a_ref[t * span:(t + 1) * span, :], preferred_element_type=F32).astype(lhs_ref.dtype)

    acc = jnp.dot(lhs_ref[...], b_ref[...], preferred_element_type=F32)
    for t in range(runs):
        for r in range(DEINT):
            o_ref[r, t * rows:(t + 1) * rows, :] = (
                acc[t * span + r * rows:t * span + (r + 1) * rows].astype(o_ref.dtype))


def matmul_deinterleaved(a, b, out_dtype, seq, tm=1024, tn=1024):
    m, k = a.shape
    _, n = b.shape
    tm, tn = _tile(seq, tm), _tile(n, tn)
    span = min(DEINT_SPAN, tm)
    rows, nt = span // DEINT, seq // tm
    perm = np.zeros((span, span), np.float32)
    tok = np.arange(span)
    perm[(tok % DEINT) * rows + tok // DEINT, tok] = 1.0
    return pl.pallas_call(
        _mm_deint_kernel, grid=(m // tm, n // tn),
        in_specs=[pl.BlockSpec((span, span), lambda i, j: (0, 0)),
                  pl.BlockSpec((tm, k), lambda i, j: (i, 0)),
                  pl.BlockSpec((k, tn), lambda i, j: (0, j))],
        out_specs=pl.BlockSpec((DEINT, tm // DEINT, tn), lambda i, j: (i // nt, i % nt, j)),
        out_shape=jax.ShapeDtypeStruct((m // seq * DEINT, seq // DEINT, n), out_dtype),
        scratch_shapes=[pltpu.VMEM((tm, k), a.dtype)],
        compiler_params=_params("parallel", "arbitrary"), name="mm_deint")(jnp.asarray(perm, a.dtype), a, b)


def matmul_nt(a, b, out_dtype, tm=1024, tn=1024):
    m, k = a.shape
    n, _ = b.shape
    tm, tn = _tile(m, tm), _tile(n, tn)
    return pl.pallas_call(
        _mm_nt_kernel, grid=(m // tm, n // tn),
        in_specs=[pl.BlockSpec((tm, k), lambda i, j: (i, 0)),
                  pl.BlockSpec((tn, k), lambda i, j: (j, 0))],
        out_specs=pl.BlockSpec((tm, tn), lambda i, j: (i, j)),
        out_shape=jax.ShapeDtypeStruct((m, n), out_dtype),
        compiler_params=_params("parallel", "parallel"), name="mm_nt")(a, b)


S5_PARAMS = 6
S5_PAIR = 2


def _s5_toeplitz_rows(kmat_ref, m_ref, ci):
    causal = (lax.broadcasted_iota(jnp.int32, (CHUNK, CHUNK), 1)
              >= lax.broadcasted_iota(jnp.int32, (CHUNK, CHUNK), 0))
    row0 = pl.multiple_of(ci * CHUNK, CHUNK)
    for co in range(S5_GROUP):
        krow = kmat_ref[pl.ds(ci * S5_GROUP + co, 1), :]
        shifted = pltpu.roll(jnp.broadcast_to(krow, (CHUNK, CHUNK)), 0, 1, stride=1, stride_axis=0)
        m_ref[pl.ds(row0, CHUNK), co * CHUNK:(co + 1) * CHUNK] = jnp.where(causal, shifted, 0.0).astype(BF16)


def _s5_prepare(params, gi, slot, kmat_ref):
    ls_ref, laml_ref, lams_ref, bt_ref, c_ref, ct_ref = params
    m_ref, wst_re_ref, wst_im_ref, wa_ref, wb_ref, ac_ref = slot
    grp = S5_GROUP
    dt = jnp.exp(ls_ref[gi])
    a_re_l = jnp.minimum(laml_ref[gi, 0:1, :], -1e-4)
    a_im_l = laml_ref[gi, 1:2, :]
    a_re_s = jnp.minimum(lams_ref[gi, :, 0:1], -1e-4)
    a_im_s = lams_ref[gi, :, 1:2]

    def cpow(a_re, a_im, n):
        mag = jnp.exp(n * (dt * a_re))
        ang = n * (dt * a_im)
        return mag * jnp.cos(ang), mag * jnp.sin(ang)

    ab_re, ab_im = cpow(a_re_l, a_im_l, 1.0)
    den = a_re_l * a_re_l + a_im_l * a_im_l
    nr = ab_re - 1.0
    co_re = (nr * a_re_l + ab_im * a_im_l) / den
    co_im = (ab_im * a_re_l - nr * a_im_l) / den
    br_t, bi_t = bt_ref[gi, 0], bt_ref[gi, 1]
    bbr = co_re * br_t - co_im * bi_t
    bbi = co_re * bi_t + co_im * br_t
    cr, ci_ = c_ref[gi, 0], c_ref[gi, 1]

    g1 = jnp.concatenate([bbr[i:i + 1] * cr - bbi[i:i + 1] * ci_ for i in range(grp)], axis=0)
    g2 = jnp.concatenate([bbr[i:i + 1] * ci_ + bbi[i:i + 1] * cr for i in range(grp)], axis=0)
    tau = lax.broadcasted_iota(jnp.int32, (S5_STATE, CHUNK), 1).astype(F32)
    e_re, e_im = cpow(a_re_s, a_im_s, tau)
    kmat_ref[...] = (jnp.dot(g1, e_re, precision=lax.Precision.HIGHEST, preferred_element_type=F32)
                     - jnp.dot(g2, e_im, precision=lax.Precision.HIGHEST, preferred_element_type=F32))

    rev = (CHUNK - 1 - lax.broadcasted_iota(jnp.int32, (CHUNK, S5_STATE), 0)).astype(F32)
    er_re, er_im = cpow(a_re_l, a_im_l, rev)
    for i in range(grp):
        wst_re_ref[i * CHUNK:(i + 1) * CHUNK, :] = (er_re * bbr[i:i + 1] - er_im * bbi[i:i + 1]).astype(BF16)
        wst_im_ref[i * CHUNK:(i + 1) * CHUNK, :] = (er_re * bbi[i:i + 1] + er_im * bbr[i:i + 1]).astype(BF16)

    t1 = (lax.broadcasted_iota(jnp.int32, (S5_STATE, CHUNK), 1) + 1).astype(F32)
    e1_re, e1_im = cpow(a_re_s, a_im_s, t1)
    cr_t, ci_t = ct_ref[gi, 0], ct_ref[gi, 1]
    for o in range(grp):
        wa_ref[:, o * CHUNK:(o + 1) * CHUNK] = (cr_t[:, o:o + 1] * e1_re - ci_t[:, o:o + 1] * e1_im).astype(BF16)
        wb_ref[:, o * CHUNK:(o + 1) * CHUNK] = (-(cr_t[:, o:o + 1] * e1_im + ci_t[:, o:o + 1] * e1_re)).astype(BF16)

    ac_re, ac_im = cpow(a_re_l, a_im_l, float(CHUNK))
    ac_ref[0:1, :] = ac_re
    ac_ref[1:2, :] = ac_im


S5_K_SLICE = 2


def _s5_kernel(*refs, batch):
    cur, nxt = refs[:S5_PARAMS], refs[S5_PARAMS:2 * S5_PARAMS]
    u_ref, o_ref, kmat_ref = refs[2 * S5_PARAMS:2 * S5_PARAMS + 3]
    rest = refs[2 * S5_PARAMS + 3:]
    slots = [rest[6 * i:6 * i + 6] for i in range(S5_PAIR)]
    ybig_ref, sre_ref, sim_ref, hre_ref, him_ref = rest[6 * S5_PAIR:]
    grp = S5_GROUP
    nc = u_ref.shape[1]
    nck = nc // batch

    @pl.when(pl.program_id(0) == 0)
    def _():
        _s5_prepare(cur, 0, slots[0], kmat_ref)

        def rows(ci, carry):
            _s5_toeplitz_rows(kmat_ref, slots[0][0], ci)
            return carry
        lax.fori_loop(0, grp, rows, 0)

    for gi in range(S5_PAIR):
        m_ref, wst_re_ref, wst_im_ref, wa_ref, wb_ref, ac_ref = slots[gi]
        following = slots[(gi + 1) % S5_PAIR]
        lhs = jnp.concatenate([u_ref[gi * grp + i].astype(BF16) for i in range(grp)], axis=1)
        sre_ref[...] = jnp.dot(lhs, wst_re_ref[...], preferred_element_type=F32)
        sim_ref[...] = jnp.dot(lhs, wst_im_ref[...], preferred_element_type=F32)
        if gi + 1 < S5_PAIR:
            _s5_prepare(cur, gi + 1, following, kmat_ref)
        else:
            _s5_prepare(nxt, 0, following, kmat_ref)
        ybig_ref[...] = jnp.zeros(ybig_ref.shape, F32)

        def contract(it, carry, gi=gi, m_ref=m_ref, following=following):
            chans = [gi * grp + it * S5_K_SLICE + j for j in range(S5_K_SLICE)]
            part = jnp.concatenate([u_ref[ch].astype(BF16) for ch in chans], axis=1)
            k0 = pl.multiple_of(it * (S5_K_SLICE * CHUNK), S5_K_SLICE * CHUNK)
            ybig_ref[...] += jnp.dot(part, m_ref[pl.ds(k0, S5_K_SLICE * CHUNK), :], preferred_element_type=F32)
            for j in range(S5_K_SLICE):
                _s5_toeplitz_rows(kmat_ref, following[0], it * S5_K_SLICE + j)
            return carry
        lax.fori_loop(0, grp // S5_K_SLICE, contract, 0)

        ac_re, ac_im = ac_ref[0:1, :], ac_ref[1:2, :]

        def carry_state(k, carry, ac_re=ac_re, ac_im=ac_im):
            new = []
            for b in range(batch):
                h_re, h_im = carry[2 * b], carry[2 * b + 1]
                row = b * nck + k
                hre_ref[pl.ds(row, 1), :] = h_re
                him_ref[pl.ds(row, 1), :] = h_im
                new.append(ac_re * h_re - ac_im * h_im + sre_ref[pl.ds(row, 1), :])
                new.append(ac_re * h_im + ac_im * h_re + sim_ref[pl.ds(row, 1), :])
            return tuple(new)

        zero = jnp.zeros((1, S5_STATE), F32)
        lax.fori_loop(0, nck, carry_state, (zero,) * (2 * batch))

        y = (ybig_ref[...]
             + jnp.dot(hre_ref[...].astype(BF16), wa_ref[...], preferred_element_type=F32)
             + jnp.dot(him_ref[...].astype(BF16), wb_ref[...], preferred_element_type=F32))
        for o in range(grp):
            o_ref[gi * grp + o] = y[:, o * CHUNK:(o + 1) * CHUNK]


def s5_core(u_t, log_step, lam_re, lam_im, b_re, b_im, c_re, c_im, batch):
    width, tokens = u_t.shape
    groups = width // S5_GROUP
    assert groups % S5_PAIR == 0
    steps = groups // S5_PAIR
    nc = tokens // CHUNK
    u3 = u_t.reshape(width, nc, CHUNK)
    ls = log_step.reshape(groups, 1, 1)
    lam_l = jnp.stack([lam_re, lam_im], axis=1)
    lam_s = jnp.stack([lam_re, lam_im], axis=2)
    b_t = jnp.stack([b_re, b_im], axis=1).transpose(0, 1, 3, 2)
    c = jnp.stack([c_re, c_im], axis=1)
    c_t = c.transpose(0, 1, 3, 2)
    params = (ls, lam_l, lam_s, b_t, c, c_t)
    gw = S5_GROUP * CHUNK
    cur = lambda a: pl.BlockSpec((S5_PAIR,) + a.shape[1:], lambda g: (g,) + (0,) * (a.ndim - 1))
    nxt = lambda a: pl.BlockSpec((1,) + a.shape[1:],
                                 lambda g: (jnp.minimum(S5_PAIR * (g + 1), groups - 1),) + (0,) * (a.ndim - 1))
    chan = pl.BlockSpec((S5_PAIR * S5_GROUP, nc, CHUNK), lambda g: (g, 0, 0))
    slot = [pltpu.VMEM((gw, gw), BF16),
            pltpu.VMEM((gw, S5_STATE), BF16), pltpu.VMEM((gw, S5_STATE), BF16),
            pltpu.VMEM((S5_STATE, gw), BF16), pltpu.VMEM((S5_STATE, gw), BF16),
            pltpu.VMEM((2, S5_STATE), F32)]
    out = pl.pallas_call(
        functools.partial(_s5_kernel, batch=batch), grid=(steps,),
        in_specs=[cur(a) for a in params] + [nxt(a) for a in params] + [chan],
        out_specs=chan,
        out_shape=jax.ShapeDtypeStruct((width, nc, CHUNK), F32),
        scratch_shapes=[pltpu.VMEM((S5_GROUP * S5_GROUP, CHUNK), F32)] + slot * S5_PAIR
        + [pltpu.VMEM((nc, gw), F32)] + [pltpu.VMEM((nc, S5_STATE), F32)] * 4,
        compiler_params=_params("arbitrary"), name="s5_core")(*params, *params, u3)
    return out.reshape(width, tokens)


def _glu_t_kernel(y_ref, u_ref, z_ref, d_ref, w_ref, b_ref, o_ref):
    y = jax.nn.gelu(y_ref[...] + d_ref[...] * u_ref[...])
    acc = jnp.dot(w_ref[...], y.astype(BF16), preferred_element_type=F32)
    z = z_ref[...]
    o_ref[...] = (y * jax.nn.sigmoid(acc + b_ref[...]) * (z * jax.nn.sigmoid(z))).astype(o_ref.dtype)


def s5_glu_t(y_t, p_t, d_skip, w_glu_t, b_glu, tm=256):
    width, tokens = y_t.shape
    tm = _tile(tokens, tm)
    col = lambda i: (0, i)
    return pl.pallas_call(
        _glu_t_kernel, grid=(tokens // tm,),
        in_specs=[pl.BlockSpec((width, tm), col), pl.BlockSpec((width, tm), col),
                  pl.BlockSpec((width, tm), lambda i: (1, i)),
                  pl.BlockSpec((width, 1), lambda i: (0, 0)),
                  pl.BlockSpec((width, width), lambda i: (0, 0)),
                  pl.BlockSpec((width, 1), lambda i: (0, 0))],
        out_specs=pl.BlockSpec((width, tm), col),
        out_shape=jax.ShapeDtypeStruct((width, tokens), BF16),
        compiler_params=_params("parallel"), name="s5_glu")(
            y_t, p_t, p_t, d_skip.reshape(width, 1), w_glu_t, b_glu.reshape(width, 1))


def _proj_conv_kernel(x_ref, wh_ref, wg_ref, wc_ref, wz_ref, cw_ref, cb_ref, o_ref, tail_ref, *, seq):
    tm = x_ref.shape[0]
    x = x_ref[...]
    v = (jnp.dot(x, wc_ref[...], preferred_element_type=F32) * jnp.dot(x, wh_ref[...], preferred_element_type=F32))
    first = (pl.program_id(1) * tm) % seq == 0
    ext = jnp.concatenate([jnp.where(first, 0.0, tail_ref[...]), v], axis=0)
    conv = (cw_ref[0:1, :] * ext[6:6 + tm] + cw_ref[1:2, :] * ext[7:7 + tm] + cw_ref[2:3, :] * v
            + cb_ref[...])
    tail_ref[...] = v[tm - 8:]
    z = jnp.dot(x, wz_ref[...], preferred_element_type=F32)
    o_ref[...] = (jnp.dot(x, wg_ref[...], preferred_element_type=F32) * conv
                  * (z * jax.nn.sigmoid(z))).astype(o_ref.dtype)


def proj_short_conv(x, w_in, col0, conv_w, conv_b, seq, tm=1024, tc=256):
    tokens, k = x.shape
    width = conv_w.shape[1]
    tm, tc = _tile(seq, tm), _tile(width, tc)
    ncb = width // tc
    seg = lambda s: pl.BlockSpec((k, tc), lambda j, i: (0, (col0 + s * width) // tc + j))
    return pl.pallas_call(
        functools.partial(_proj_conv_kernel, seq=seq), grid=(ncb, tokens // tm),
        in_specs=[pl.BlockSpec((tm, k), lambda j, i: (i, 0)), seg(0), seg(1), seg(2), seg(3),
                  pl.BlockSpec((CONV_K, tc), lambda j, i: (0, j)),
                  pl.BlockSpec((1, tc), lambda j, i: (0, j))],
        out_specs=pl.BlockSpec((tm, tc), lambda j, i: (i, j)),
        out_shape=jax.ShapeDtypeStruct((tokens, width), BF16),
        scratch_shapes=[pltpu.VMEM((8, tc), F32)],
        compiler_params=_params("parallel", "arbitrary"), name="proj_short_conv")(
            x, w_in, w_in, w_in, w_in, conv_w, conv_b.reshape(1, width))


def _out0_kernel(ya_ref, yb_ref, wa_ref, wb_ref, o_ref):
    acc = lax.dot_general(ya_ref[...], wa_ref[...], (((0,), (0,)), ((), ())), preferred_element_type=F32)
    o_ref[...] = acc + jnp.dot(yb_ref[...], wb_ref[...], preferred_element_type=F32)


def out_proj0(ya_t, yb, w_out, tm=1024, tn=1024):
    wa_rows, tokens = ya_t.shape
    wb_rows = yb.shape[1]
    n = w_out.shape[1]
    tm, tn = _tile(tokens, tm), _tile(n, tn)
    return pl.pallas_call(
        _out0_kernel, grid=(tokens // tm, n // tn),
        in_specs=[pl.BlockSpec((wa_rows, tm), lambda i, j: (0, i)),
                  pl.BlockSpec((tm, wb_rows), lambda i, j: (i, 0)),
                  pl.BlockSpec((wa_rows, tn), lambda i, j: (0, j)),
                  pl.BlockSpec((wb_rows, tn), lambda i, j: (wa_rows // wb_rows, j))],
        out_specs=pl.BlockSpec((tm, tn), lambda i, j: (i, j)),
        out_shape=jax.ShapeDtypeStruct((tokens, n), F32),
        compiler_params=_params("parallel", "parallel"), name="out_proj0")(ya_t, yb, w_out, w_out)


def _ln_kernel(h_ref, mix_ref, g_ref, b_ref, o_ref, ob_ref):
    y = DEEPNORM_ALPHA * h_ref[...] + mix_ref[...]
    mu = jnp.mean(y, axis=-1, keepdims=True)
    yc = y - mu
    var = jnp.mean(yc * yc, axis=-1, keepdims=True)
    out = yc * lax.rsqrt(var + LN_EPS) * g_ref[...] + b_ref[...]
    o_ref[...] = out
    ob_ref[...] = out.astype(BF16)


def deepnorm_ln(h, mix, g, b, tm=256):
    tokens, d = h.shape
    tm = _tile(tokens, tm)
    row = pl.BlockSpec((tm, d), lambda i: (i, 0))
    vec = pl.BlockSpec((1, d), lambda i: (0, 0))
    return pl.pallas_call(
        _ln_kernel, grid=(tokens // tm,),
        in_specs=[row, row, vec, vec], out_specs=[row, row],
        out_shape=[jax.ShapeDtypeStruct((tokens, d), F32), jax.ShapeDtypeStruct((tokens, d), BF16)],
        compiler_params=_params("parallel"), name="deepnorm_ln")(h, mix, g.reshape(1, d), b.reshape(1, d))


def _gate_kernel(hb_ref, w_ref, bg_ref, h_ref, p_ref, wp_ref, o_ref, ob_ref):
    gate = jax.nn.sigmoid(jnp.dot(hb_ref[...], w_ref[...], preferred_element_type=F32) + bg_ref[...])
    out = h_ref[...] + gate * jnp.dot(p_ref[...], wp_ref[...], preferred_element_type=F32)
    o_ref[...] = out
    ob_ref[...] = out.astype(BF16)


def ple_gate(h, hb, w_gate, b_gate, p, w_ple, tm=1024, tn=512):
    tokens, d = h.shape
    pd = p.shape[1]
    tm, tn = _tile(tokens, tm), _tile(d, tn)
    return pl.pallas_call(
        _gate_kernel, grid=(tokens // tm, d // tn),
        in_specs=[pl.BlockSpec((tm, d), lambda i, j: (i, 0)),
                  pl.BlockSpec((d, tn), lambda i, j: (0, j)),
                  pl.BlockSpec((1, tn), lambda i, j: (0, j)),
                  pl.BlockSpec((tm, tn), lambda i, j: (i, j)),
                  pl.BlockSpec((tm, pd), lambda i, j: (i, 0)),
                  pl.BlockSpec((pd, tn), lambda i, j: (0, j))],
        out_specs=[pl.BlockSpec((tm, tn), lambda i, j: (i, j))] * 2,
        out_shape=[jax.ShapeDtypeStruct((tokens, d), F32), jax.ShapeDtypeStruct((tokens, d), BF16)],
        compiler_params=_params("parallel", "parallel"), name="ple_gate")(
            hb, w_gate, b_gate.reshape(1, d), h, p, w_ple)


def _distance_tables(dil):
    groups = DEINT // dil
    rows = ATT_GROUP_ROWS[dil]
    f_q, a_q = np.divmod(np.arange(groups * rows), rows)
    iq = groups * a_q + f_q

    def table(k_rows, a0):
        f_k, a_k = np.divmod(np.arange(groups * k_rows), k_rows)
        delta = iq[:, None] - (groups * (a_k + a0) + f_k)[None, :]
        return np.where((delta >= 0) & (delta <= SPAN), dil * delta, FAR).astype(np.float32)

    return table(2 * rows, -rows), table(rows, 0)


def _attn_kernel(sl_ref, *refs):
    nb = len(DILATED)
    dist_refs, unperm_ref = refs[:2 * nb], refs[2 * nb]
    q_ref, k_ref, v_ref, z_ref, o_ref, m_ref, l_ref, acc_ref = refs[2 * nb + 1:2 * nb + 9]
    bias_refs = refs[2 * nb + 9:]
    n_res = q_ref.shape[1]
    slope = sl_ref[0]
    for d_ref, b_ref in zip(dist_refs, bias_refs):
        b_ref[...] = -slope * d_ref[...]

    def rows_of(ref, res, starts, size):
        if len(ref.shape) == 3:
            return jnp.concatenate([ref[r, pl.ds(s, size), :] for r, s in zip(res, starts)], axis=0)
        return jnp.concatenate([ref[pl.ds(r * n_res + s, size), :] for r, s in zip(res, starts)], axis=0)

    def block_group(dil, c, items, tabs, fresh, final):
        groups, rows = DEINT // dil, ATT_GROUP_ROWS[dil]
        align = min(rows, 16)
        res = [dil * f + c for f in range(groups)]
        q_at = [[pl.multiple_of(rows * j, align)] * groups for j, _ in items]
        k_at = [qa if first else [pl.multiple_of(rows * (j - 1), align)] * groups
                for qa, (j, first) in zip(q_at, items)]
        k_rows = [rows if first else 2 * rows for _, first in items]
        bias = [tabs[1] if first else tabs[0] for _, first in items]
        n = range(len(items))
        ones = jnp.ones((2 * rows * groups, HEAD_DIM), BF16)
        q = [rows_of(q_ref, res, q_at[i], rows) for i in n]
        k = [rows_of(k_ref, res, k_at[i], k_rows[i]) for i in n]
        v1 = [jnp.concatenate([rows_of(v_ref, res, k_at[i], k_rows[i]), ones[:k_rows[i] * groups]], axis=1)
              for i in n]
        if not fresh:
            m_prev = [rows_of(m_ref, res, q_at[i], rows) for i in n]
            l_prev = [rows_of(l_ref, res, q_at[i], rows) for i in n]
            acc_prev = [rows_of(acc_ref, res, q_at[i], rows) for i in n]
        s = [lax.dot_general(q[i], k[i], (((1,), (1,)), ((), ())), preferred_element_type=F32)
             * (HEAD_DIM ** -0.5) + bias[i][...] for i in n]
        m_new = [jnp.broadcast_to(jnp.max(s[i], axis=-1, keepdims=True), (rows * groups, HEAD_DIM)) for i in n]
        if not fresh:
            m_new = [jnp.maximum(m_prev[i], m_new[i]) for i in n]
        p = [jnp.concatenate([jnp.exp(s[i][:, t:t + HEAD_DIM] - m_new[i])
                              for t in range(0, s[i].shape[1], HEAD_DIM)], axis=1).astype(BF16) for i in n]
        pv = [jnp.dot(p[i], v1[i], preferred_element_type=F32) for i in n]
        acc_new = [pv[i][:, :HEAD_DIM] for i in n]
        l_new = [pv[i][:, HEAD_DIM:] for i in n]
        if not fresh:
            alpha = [jnp.exp(m_prev[i] - m_new[i]) for i in n]
            l_new = [alpha[i] * l_prev[i] + l_new[i] for i in n]
            acc_new = [alpha[i] * acc_prev[i] + acc_new[i] for i in n]
        for i in n:
            if final:
                span = rows * groups
                at = pl.multiple_of(span * items[i][0], span)
                o = jnp.dot(unperm_ref[...], (acc_new[i] / l_new[i]).astype(BF16), preferred_element_type=F32)
                z = z_ref[pl.ds(at, span), :]
                o_ref[pl.ds(at, span), :] = (o * (z * jax.nn.sigmoid(z))).astype(o_ref.dtype)
                continue
            for f in range(groups):
                at = res[f] * n_res + q_at[i][f]
                part = slice(f * rows, (f + 1) * rows)
                m_ref[pl.ds(at, rows), :] = m_new[i][part]
                l_ref[pl.ds(at, rows), :] = l_new[i][part]
                acc_ref[pl.ds(at, rows), :] = acc_new[i][part]

    order = sorted(range(nb), key=lambda i: -DILATED[i][1])
    for pos, idx in enumerate(order):
        dil = DILATED[idx][1]
        fresh, final = pos == 0, pos == nb - 1
        tabs = (bias_refs[2 * idx], bias_refs[2 * idx + 1])
        n_blocks = n_res // ATT_GROUP_ROWS[dil]
        per = min(ATT_BLOCKS_PER_STEP[dil], n_blocks)
        assert n_blocks % per == 0

        def subsequence(c, carry, dil=dil, fresh=fresh, final=final, tabs=tabs, n_blocks=n_blocks, per=per):
            block_group(dil, c, [(j, j == 0) for j in range(per)], tabs, fresh, final)

            def later(g, carry2):
                block_group(dil, c, [(g * per + j, False) for j in range(per)], tabs, fresh, final)
                return carry2
            lax.fori_loop(1, n_blocks // per, later, 0)
            return carry
        lax.fori_loop(0, dil, subsequence, 0)


def dilated_attention(qkv, z, batch, seq):
    tokens, width = z.shape
    heads = width // HEAD_DIM
    n_res = seq // DEINT
    slopes = jnp.asarray(np.exp2(-8.0 * np.arange(1, heads + 1) / heads).astype(np.float32)).reshape(heads, 1, 1)
    tables = [jnp.asarray(t) for _, dil in DILATED for t in _distance_tables(dil)]
    assert min(d for _, d in DILATED) == 1
    rows = ATT_GROUP_ROWS[1]
    tok = np.arange(DEINT * rows)
    unperm = np.zeros((DEINT * rows, DEINT * rows), np.float32)
    unperm[tok, (tok % DEINT) * rows + tok // DEINT] = 1.0
    whole = lambda t: pl.BlockSpec(t.shape, lambda b, h: (0, 0))
    grouped = lambda off: pl.BlockSpec((DEINT, n_res, HEAD_DIM), lambda b, h: (b, 0, off + h))
    natural = pl.BlockSpec((seq, HEAD_DIM), lambda b, h: (b, h))
    return pl.pallas_call(
        _attn_kernel, grid=(batch, heads),
        in_specs=[pl.BlockSpec((1, 1, 1), lambda b, h: (h, 0, 0))] + [whole(t) for t in tables]
        + [whole(unperm), grouped(0), grouped(heads), grouped(2 * heads), natural],
        out_specs=natural,
        out_shape=jax.ShapeDtypeStruct((tokens, width), BF16),
        scratch_shapes=[pltpu.VMEM((seq, HEAD_DIM), F32)] * 3
        + [pltpu.VMEM(t.shape, F32) for t in tables],
        compiler_params=_params("parallel", "parallel"),
        name="dilated_attn")(slopes, *tables, jnp.asarray(unperm, BF16), qkv, qkv, qkv, z)


def kernel(x, p, ab_w_in, ab_lambda_re, ab_lambda_im, ab_log_step, ab_b_re, ab_b_im, ab_c_re, ab_c_im, ab_d,
           ab_w_glu, ab_b_glu, ab_conv_w, ab_conv_b, ab_w_out, at_w_in, at_w_out, ln_g, ln_b, ple_w,
           ple_gate_w, ple_gate_b):
    batch, seq, d = x.shape
    tokens = batch * seq
    s5w = d // 2
    assert seq % (DEINT * SPAN) == 0 and seq % CHUNK == 0 and s5w % S5_GROUP == 0
    h = x.reshape(tokens, d)
    hb = h.astype(BF16)
    pb = p.reshape(DEPTH, tokens, PLE_DIM).astype(BF16)

    w_in = ab_w_in[0].astype(BF16)
    proj_t = matmul_nt(w_in[:, :2 * s5w].T, hb, F32)
    y_t = s5_core(proj_t[:s5w], ab_log_step[0], ab_lambda_re[0], ab_lambda_im[0], ab_b_re[0], ab_b_im[0],
                  ab_c_re[0], ab_c_im[0], batch)
    ya_t = s5_glu_t(y_t, proj_t, ab_d[0], ab_w_glu[0].T.astype(BF16), ab_b_glu[0])
    yb = proj_short_conv(hb, w_in, 2 * s5w, ab_conv_w[0], ab_conv_b[0], seq)
    mix = out_proj0(ya_t, yb, ab_w_out[0].astype(BF16))
    h, hb = deepnorm_ln(h, mix, ln_g[0], ln_b[0])
    h, hb = ple_gate(h, hb, ple_gate_w[0].astype(BF16), ple_gate_b[0], pb[0], ple_w[0].astype(BF16))

    w_in = at_w_in[0].astype(BF16)
    qkv = matmul_deinterleaved(hb, w_in[:, :3 * d], BF16, seq)
    z = matmul_nn(hb, w_in[:, 3 * d:], F32)
    og = dilated_attention(qkv, z, batch, seq)
    mix = matmul_nn(og, at_w_out[0].astype(BF16), F32)
    h, hb = deepnorm_ln(h, mix, ln_g[1], ln_b[1])
    h, _ = ple_gate(h, hb, ple_gate_w[1].astype(BF16), ple_gate_b[1], pb[1], ple_w[1].astype(BF16))
    return h.reshape(batch, seq, d)
```

```python
import functools
import math

import jax
import jax.numpy as jnp
import numpy as np
from jax import lax
from jax.experimental import pallas as pl
from jax.experimental.pallas import tpu as pltpu

F32 = jnp.float32
BF16 = jnp.bfloat16

V7X_VMEM_LIMIT_BYTES = 56 * 1024 * 1024
LANES = 128
S5_GROUP = 16
S5_STATE = 64
CHUNK = LANES
CONV_K = 3
HEAD_DIM = 128
PLE_DIM = 256
LN_EPS = 1e-5
DEPTH = 2
DEEPNORM_ALPHA = (2.0 * DEPTH) ** 0.25
DILATED = ((128, 1), (512, 4), (2048, 16))
SPAN = 128
DEINT = max(d for _, d in DILATED)
ATT_GROUP_ROWS = {16: 128, 4: 32, 1: 8}
FAR = 1e32
ATT_BLOCKS_PER_STEP = 8
assert all(w // d == SPAN for w, d in DILATED)


def _params(*sem):
    return pltpu.CompilerParams(dimension_semantics=sem, vmem_limit_bytes=V7X_VMEM_LIMIT_BYTES)


def _tile(n, pref):
    return pref if n % pref == 0 else n


def _mm_nn_kernel(a_ref, b_ref, o_ref):
    o_ref[...] = jnp.dot(a_ref[...], b_ref[...], preferred_element_type=F32).astype(o_ref.dtype)


def _mm_heads_out_kernel(a_ref, b_ref, o_ref):
    acc = jnp.dot(a_ref[...], b_ref[...], preferred_element_type=F32).astype(o_ref.dtype)
    for h in range(o_ref.shape[0]):
        o_ref[h] = acc[:, h * HEAD_DIM:(h + 1) * HEAD_DIM]


def _mm_heads_in_kernel(a_ref, b_ref, o_ref):
    lhs = jnp.concatenate([a_ref[h] for h in range(a_ref.shape[0])], axis=1)
    o_ref[...] = jnp.dot(lhs, b_ref[...], preferred_element_type=F32).astype(o_ref.dtype)


def _mm_nt_kernel(a_ref, b_ref, o_ref):
    o_ref[...] = lax.dot_general(a_ref[...], b_ref[...], (((1,), (1,)), ((), ())),
                                 preferred_element_type=F32).astype(o_ref.dtype)


def matmul_nn(a, b, out_dtype, tm=1024, tn=1024):
    m, k = a.shape
    _, n = b.shape
    tm, tn = _tile(m, tm), _tile(n, tn)
    return pl.pallas_call(
        _mm_nn_kernel, grid=(m // tm, n // tn),
        in_specs=[pl.BlockSpec((tm, k), lambda i, j: (i, 0)),
                  pl.BlockSpec((k, tn), lambda i, j: (0, j))],
        out_specs=pl.BlockSpec((tm, tn), lambda i, j: (i, j)),
        out_shape=jax.ShapeDtypeStruct((m, n), out_dtype),
        compiler_params=_params("parallel", "parallel"), name="mm_nn")(a, b)


DEINT_SPAN = 512


def matmul_heads_out(a, b, col0, n, out_dtype, tm=1024, tn=1024):
    m, k = a.shape
    tm, tn = _tile(m, tm), _tile(n, tn)
    assert col0 % tn == 0
    return pl.pallas_call(
        _mm_heads_out_kernel, grid=(m // tm, n // tn),
        in_specs=[pl.BlockSpec((tm, k), lambda i, j: (i, 0)),
                  pl.BlockSpec((k, tn), lambda i, j: (0, col0 // tn + j))],
        out_specs=pl.BlockSpec((tn // HEAD_DIM, tm, HEAD_DIM), lambda i, j: (j, i, 0)),
        out_shape=jax.ShapeDtypeStruct((n // HEAD_DIM, m, HEAD_DIM), out_dtype),
        compiler_params=_params("parallel", "parallel"), name="mm_heads_out")(a, b)


def matmul_heads_in(a, b, out_dtype, tm=1024, tn=1024):
    heads, m, _ = a.shape
    k, n = b.shape
    tm, tn = _tile(m, tm), _tile(n, tn)
    return pl.pallas_call(
        _mm_heads_in_kernel, grid=(m // tm, n // tn),
        in_specs=[pl.BlockSpec((heads, tm, HEAD_DIM), lambda i, j: (0, i, 0)),
                  pl.BlockSpec((k, tn), lambda i, j: (0, j))],
        out_specs=pl.BlockSpec((tm, tn), lambda i, j: (i, j)),
        out_shape=jax.ShapeDtypeStruct((m, n), out_dtype),
        compiler_params=_params("parallel", "parallel"), name="mm_heads_in")(a, b)


def _mm_deint_kernel(perm_ref, a_ref, b_ref, o_ref, lhs_ref):
    span = perm_ref.shape[0]
    runs = a_ref.shape[0] // span
    rows = span // DEINT

    @pl.when(pl.program_id(1) == 0)
    def _():
        for t in range(runs):
            lhs_ref[t * span:(t + 1) * span, :] = jnp.dot(
                perm_ref[...], a_ref[t * span:(t + 1) * span, :], preferred_element_type=F32).astype(lhs_ref.dtype)

    acc = jnp.dot(lhs_ref[...], b_ref[...], preferred_element_type=F32).astype(o_ref.dtype)
    for t in range(runs):
        for r in range(DEINT):
            for h in range(o_ref.shape[1]):
                o_ref[r, h, t * rows:(t + 1) * rows, :] = (
                    acc[t * span + r * rows:t * span + (r + 1) * rows, h * HEAD_DIM:(h + 1) * HEAD_DIM])


def matmul_deinterleaved(a, b, n, out_dtype, seq, tm=1024, tn=1024):
    m, k = a.shape
    tm, tn = _tile(seq, tm), _tile(n, tn)
    span = min(DEINT_SPAN, tm)
    rows, nt = span // DEINT, seq // tm
    perm = np.zeros((span, span), np.float32)
    tok = np.arange(span)
    perm[(tok % DEINT) * rows + tok // DEINT, tok] = 1.0
    return pl.pallas_call(
        _mm_deint_kernel, grid=(m // tm, n // tn),
        in_specs=[pl.BlockSpec((span, span), lambda i, j: (0, 0)),
                  pl.BlockSpec((tm, k), lambda i, j: (i, 0)),
                  pl.BlockSpec((k, tn), lambda i, j: (0, j))],
        out_specs=pl.BlockSpec((DEINT, tn // HEAD_DIM, tm // DEINT, HEAD_DIM), lambda i, j: (i // nt, j, i % nt, 0)),
        out_shape=jax.ShapeDtypeStruct((m // seq * DEINT, n // HEAD_DIM, seq // DEINT, HEAD_DIM), out_dtype),
        scratch_shapes=[pltpu.VMEM((tm, k), a.dtype)],
        compiler_params=_params("parallel", "arbitrary"), name="mm_deint")(jnp.asarray(perm, a.dtype), a, b)


def matmul_nt(a, b, out_dtype, tm=1024, tn=1024):
    m, k = a.shape
    n, _ = b.shape
    tm, tn = _tile(m, tm), _tile(n, tn)
    return pl.pallas_call(
        _mm_nt_kernel, grid=(m // tm, n // tn),
        in_specs=[pl.BlockSpec((tm, k), lambda i, j: (i, 0)),
                  pl.BlockSpec((tn, k), lambda i, j: (j, 0))],
        out_specs=pl.BlockSpec((tm, tn), lambda i, j: (i, j)),
        out_shape=jax.ShapeDtypeStruct((m, n), out_dtype),
        compiler_params=_params("parallel", "parallel"), name="mm_nt")(a, b)


S5_PARAMS = 6
S5_PAIR = 2


def _s5_toeplitz_rows(kmat_ref, m_ref, ci):
    causal = (lax.broadcasted_iota(jnp.int32, (CHUNK, CHUNK), 1)
              >= lax.broadcasted_iota(jnp.int32, (CHUNK, CHUNK), 0))
    row0 = pl.multiple_of(ci * CHUNK, CHUNK)
    for co in range(S5_GROUP):
        krow = kmat_ref[pl.ds(ci * S5_GROUP + co, 1), :]
        shifted = pltpu.roll(jnp.broadcast_to(krow, (CHUNK, CHUNK)), 0, 1, stride=1, stride_axis=0)
        m_ref[pl.ds(row0, CHUNK), co * CHUNK:(co + 1) * CHUNK] = jnp.where(causal, shifted, 0.0).astype(BF16)


def _s5_prepare(params, gi, slot, kmat_ref):
    ls_ref, laml_ref, lams_ref, bt_ref, c_ref, ct_ref = params
    m_ref, wst_re_ref, wst_im_ref, wa_ref, wb_ref, ac_ref = slot
    grp = S5_GROUP
    dt = jnp.exp(ls_ref[gi])
    a_re_l = jnp.minimum(laml_ref[gi, 0:1, :], -1e-4)
    a_im_l = laml_ref[gi, 1:2, :]
    a_re_s = jnp.minimum(lams_ref[gi, :, 0:1], -1e-4)
    a_im_s = lams_ref[gi, :, 1:2]

    def cpow(a_re, a_im, n):
        mag = jnp.exp(n * (dt * a_re))
        ang = n * (dt * a_im)
        return mag * jnp.cos(ang), mag * jnp.sin(ang)

    ab_re, ab_im = cpow(a_re_l, a_im_l, 1.0)
    den = a_re_l * a_re_l + a_im_l * a_im_l
    nr = ab_re - 1.0
    co_re = (nr * a_re_l + ab_im * a_im_l) / den
    co_im = (ab_im * a_re_l - nr * a_im_l) / den
    br_t, bi_t = bt_ref[gi, 0], bt_ref[gi, 1]
    bbr = co_re * br_t - co_im * bi_t
    bbi = co_re * bi_t + co_im * br_t
    cr, ci_ = c_ref[gi, 0], c_ref[gi, 1]

    g1 = jnp.concatenate([bbr[i:i + 1] * cr - bbi[i:i + 1] * ci_ for i in range(grp)], axis=0)
    g2 = jnp.concatenate([bbr[i:i + 1] * ci_ + bbi[i:i + 1] * cr for i in range(grp)], axis=0)
    tau = lax.broadcasted_iota(jnp.int32, (S5_STATE, CHUNK), 1).astype(F32)
    e_re, e_im = cpow(a_re_s, a_im_s, tau)
    kmat_ref[...] = (jnp.dot(g1, e_re, precision=lax.Precision.HIGHEST, preferred_element_type=F32)
                     - jnp.dot(g2, e_im, precision=lax.Precision.HIGHEST, preferred_element_type=F32))

    rev = (CHUNK - 1 - lax.broadcasted_iota(jnp.int32, (CHUNK, S5_STATE), 0)).astype(F32)
    er_re, er_im = cpow(a_re_l, a_im_l, rev)
    for i in range(grp):
        wst_re_ref[i * CHUNK:(i + 1) * CHUNK, :] = (er_re * bbr[i:i + 1] - er_im * bbi[i:i + 1]).astype(BF16)
        wst_im_ref[i * CHUNK:(i + 1) * CHUNK, :] = (er_re * bbi[i:i + 1] + er_im * bbr[i:i + 1]).astype(BF16)

    t1 = (lax.broadcasted_iota(jnp.int32, (S5_STATE, CHUNK), 1) + 1).astype(F32)
    e1_re, e1_im = cpow(a_re_s, a_im_s, t1)
    cr_t, ci_t = ct_ref[gi, 0], ct_ref[gi, 1]
    for o in range(grp):
        wa_ref[:, o * CHUNK:(o + 1) * CHUNK] = (cr_t[:, o:o + 1] * e1_re - ci_t[:, o:o + 1] * e1_im).astype(BF16)
        wb_ref[:, o * CHUNK:(o + 1) * CHUNK] = (-(cr_t[:, o:o + 1] * e1_im + ci_t[:, o:o + 1] * e1_re)).astype(BF16)

    ac_re, ac_im = cpow(a_re_l, a_im_l, float(CHUNK))
    ac_ref[0:1, :] = ac_re
    ac_ref[1:2, :] = ac_im


S5_K_SLICE = 2


def _s5_kernel(*refs, batch):
    cur, nxt = refs[:S5_PARAMS], refs[S5_PARAMS:2 * S5_PARAMS]
    u_ref, o_ref, kmat_ref = refs[2 * S5_PARAMS:2 * S5_PARAMS + 3]
    rest = refs[2 * S5_PARAMS + 3:]
    slots = [rest[6 * i:6 * i + 6] for i in range(S5_PAIR)]
    ybig_ref, sre_ref, sim_ref, hre_ref, him_ref = rest[6 * S5_PAIR:]
    grp = S5_GROUP
    nc = u_ref.shape[1]
    nck = nc // batch

    @pl.when(pl.program_id(0) == 0)
    def _():
        _s5_prepare(cur, 0, slots[0], kmat_ref)

        def rows(ci, carry):
            _s5_toeplitz_rows(kmat_ref, slots[0][0], ci)
            return carry
        lax.fori_loop(0, grp, rows, 0)

    for gi in range(S5_PAIR):
        m_ref, wst_re_ref, wst_im_ref, wa_ref, wb_ref, ac_ref = slots[gi]
        following = slots[(gi + 1) % S5_PAIR]
        lhs = jnp.concatenate([u_ref[gi * grp + i].astype(BF16) for i in range(grp)], axis=1)
        sre_ref[...] = jnp.dot(lhs, wst_re_ref[...], preferred_element_type=F32)
        sim_ref[...] = jnp.dot(lhs, wst_im_ref[...], preferred_element_type=F32)
        if gi + 1 < S5_PAIR:
            _s5_prepare(cur, gi + 1, following, kmat_ref)
        else:
            _s5_prepare(nxt, 0, following, kmat_ref)
        ybig_ref[...] = jnp.zeros(ybig_ref.shape, F32)

        def contract(it, carry, gi=gi, m_ref=m_ref, following=following):
            chans = [gi * grp + it * S5_K_SLICE + j for j in range(S5_K_SLICE)]
            part = jnp.concatenate([u_ref[ch].astype(BF16) for ch in chans], axis=1)
            k0 = pl.multiple_of(it * (S5_K_SLICE * CHUNK), S5_K_SLICE * CHUNK)
            ybig_ref[...] += jnp.dot(part, m_ref[pl.ds(k0, S5_K_SLICE * CHUNK), :], preferred_element_type=F32)
            for j in range(S5_K_SLICE):
                _s5_toeplitz_rows(kmat_ref, following[0], it * S5_K_SLICE + j)
            return carry
        lax.fori_loop(0, grp // S5_K_SLICE, contract, 0)

        ac_re, ac_im = ac_ref[0:1, :], ac_ref[1:2, :]

        def carry_state(k, carry, ac_re=ac_re, ac_im=ac_im):
            new = []
            for b in range(batch):
                h_re, h_im = carry[2 * b], carry[2 * b + 1]
                row = b * nck + k
                hre_ref[pl.ds(row, 1), :] = h_re
                him_ref[pl.ds(row, 1), :] = h_im
                new.append(ac_re * h_re - ac_im * h_im + sre_ref[pl.ds(row, 1), :])
                new.append(ac_re * h_im + ac_im * h_re + sim_ref[pl.ds(row, 1), :])
            return tuple(new)

        zero = jnp.zeros((1, S5_STATE), F32)
        lax.fori_loop(0, nck, carry_state, (zero,) * (2 * batch))

        y = (ybig_ref[...]
             + jnp.dot(hre_ref[...].astype(BF16), wa_ref[...], preferred_element_type=F32)
             + jnp.dot(him_ref[...].astype(BF16), wb_ref[...], preferred_element_type=F32))
        for o in range(grp):
            o_ref[gi * grp + o] = y[:, o * CHUNK:(o + 1) * CHUNK]


def s5_core(u_t, log_step, lam_re, lam_im, b_re, b_im, c_re, c_im, batch):
    width, tokens = u_t.shape
    groups = width // S5_GROUP
    assert groups % S5_PAIR == 0
    steps = groups // S5_PAIR
    nc = tokens // CHUNK
    u3 = u_t.reshape(width, nc, CHUNK)
    ls = log_step.reshape(groups, 1, 1)
    lam_l = jnp.stack([lam_re, lam_im], axis=1)
    lam_s = jnp.stack([lam_re, lam_im], axis=2)
    b_t = jnp.stack([b_re, b_im], axis=1).transpose(0, 1, 3, 2)
    c = jnp.stack([c_re, c_im], axis=1)
    c_t = c.transpose(0, 1, 3, 2)
    params = (ls, lam_l, lam_s, b_t, c, c_t)
    gw = S5_GROUP * CHUNK
    cur = lambda a: pl.BlockSpec((S5_PAIR,) + a.shape[1:], lambda g: (g,) + (0,) * (a.ndim - 1))
    nxt = lambda a: pl.BlockSpec((1,) + a.shape[1:],
                                 lambda g: (jnp.minimum(S5_PAIR * (g + 1), groups - 1),) + (0,) * (a.ndim - 1))
    chan = pl.BlockSpec((S5_PAIR * S5_GROUP, nc, CHUNK), lambda g: (g, 0, 0))
    slot = [pltpu.VMEM((gw, gw), BF16),
            pltpu.VMEM((gw, S5_STATE), BF16), pltpu.VMEM((gw, S5_STATE), BF16),
            pltpu.VMEM((S5_STATE, gw), BF16), pltpu.VMEM((S5_STATE, gw), BF16),
            pltpu.VMEM((2, S5_STATE), F32)]
    out = pl.pallas_call(
        functools.partial(_s5_kernel, batch=batch), grid=(steps,),
        in_specs=[cur(a) for a in params] + [nxt(a) for a in params] + [chan],
        out_specs=chan,
        out_shape=jax.ShapeDtypeStruct((width, nc, CHUNK), F32),
        scratch_shapes=[pltpu.VMEM((S5_GROUP * S5_GROUP, CHUNK), F32)] + slot * S5_PAIR
        + [pltpu.VMEM((nc, gw), F32)] + [pltpu.VMEM((nc, S5_STATE), F32)] * 4,
        compiler_params=_params("arbitrary"), name="s5_core")(*params, *params, u3)
    return out.reshape(width, tokens)


def _glu_t_kernel(y_ref, u_ref, z_ref, d_ref, w_ref, b_ref, o_ref):
    y = jax.nn.gelu(y_ref[...] + d_ref[...] * u_ref[...])
    acc = jnp.dot(w_ref[...], y.astype(BF16), preferred_element_type=F32)
    z = z_ref[...]
    o_ref[...] = (y * jax.nn.sigmoid(acc + b_ref[...]) * (z * jax.nn.sigmoid(z))).astype(o_ref.dtype)


def s5_glu_t(y_t, p_t, d_skip, w_glu_t, b_glu, tm=256):
    width, tokens = y_t.shape
    tm = _tile(tokens, tm)
    col = lambda i: (0, i)
    return pl.pallas_call(
        _glu_t_kernel, grid=(tokens // tm,),
        in_specs=[pl.BlockSpec((width, tm), col), pl.BlockSpec((width, tm), col),
                  pl.BlockSpec((width, tm), lambda i: (1, i)),
                  pl.BlockSpec((width, 1), lambda i: (0, 0)),
                  pl.BlockSpec((width, width), lambda i: (0, 0)),
                  pl.BlockSpec((width, 1), lambda i: (0, 0))],
        out_specs=pl.BlockSpec((width, tm), col),
        out_shape=jax.ShapeDtypeStruct((width, tokens), BF16),
        compiler_params=_params("parallel"), name="s5_glu")(
            y_t, p_t, p_t, d_skip.reshape(width, 1), w_glu_t, b_glu.reshape(width, 1))


def _proj_conv_kernel(x_ref, wh_ref, wg_ref, wc_ref, wz_ref, cw_ref, cb_ref, o_ref, tail_ref, *, seq):
    tm = x_ref.shape[0]
    x = x_ref[...]
    v = (jnp.dot(x, wc_ref[...], preferred_element_type=F32) * jnp.dot(x, wh_ref[...], preferred_element_type=F32))
    first = (pl.program_id(1) * tm) % seq == 0
    ext = jnp.concatenate([jnp.where(first, 0.0, tail_ref[...]), v], axis=0)
    conv = (cw_ref[0:1, :] * ext[6:6 + tm] + cw_ref[1:2, :] * ext[7:7 + tm] + cw_ref[2:3, :] * v
            + cb_ref[...])
    tail_ref[...] = v[tm - 8:]
    z = jnp.dot(x, wz_ref[...], preferred_element_type=F32)
    o_ref[...] = (jnp.dot(x, wg_ref[...], preferred_element_type=F32) * conv
                  * (z * jax.nn.sigmoid(z))).astype(o_ref.dtype)


def proj_short_conv(x, w_in, col0, conv_w, conv_b, seq, tm=1024, tc=256):
    tokens, k = x.shape
    width = conv_w.shape[1]
    tm, tc = _tile(seq, tm), _tile(width, tc)
    ncb = width // tc
    seg = lambda s: pl.BlockSpec((k, tc), lambda j, i: (0, (col0 + s * width) // tc + j))
    return pl.pallas_call(
        functools.partial(_proj_conv_kernel, seq=seq), grid=(ncb, tokens // tm),
        in_specs=[pl.BlockSpec((tm, k), lambda j, i: (i, 0)), seg(0), seg(1), seg(2), seg(3),
                  pl.BlockSpec((CONV_K, tc), lambda j, i: (0, j)),
                  pl.BlockSpec((1, tc), lambda j, i: (0, j))],
        out_specs=pl.BlockSpec((tm, tc), lambda j, i: (i, j)),
        out_shape=jax.ShapeDtypeStruct((tokens, width), BF16),
        scratch_shapes=[pltpu.VMEM((8, tc), F32)],
        compiler_params=_params("parallel", "arbitrary"), name="proj_short_conv")(
            x, w_in, w_in, w_in, w_in, conv_w, conv_b.reshape(1, width))


def _out0_kernel(ya_ref, yb_ref, wa_ref, wb_ref, o_ref):
    acc = lax.dot_general(ya_ref[...], wa_ref[...], (((0,), (0,)), ((), ())), preferred_element_type=F32)
    o_ref[...] = acc + jnp.dot(yb_ref[...], wb_ref[...], preferred_element_type=F32)


def out_proj0(ya_t, yb, w_out, tm=1024, tn=1024):
    wa_rows, tokens = ya_t.shape
    wb_rows = yb.shape[1]
    n = w_out.shape[1]
    tm, tn = _tile(tokens, tm), _tile(n, tn)
    return pl.pallas_call(
        _out0_kernel, grid=(tokens // tm, n // tn),
        in_specs=[pl.BlockSpec((wa_rows, tm), lambda i, j: (0, i)),
                  pl.BlockSpec((tm, wb_rows), lambda i, j: (i, 0)),
                  pl.BlockSpec((wa_rows, tn), lambda i, j: (0, j)),
                  pl.BlockSpec((wb_rows, tn), lambda i, j: (wa_rows // wb_rows, j))],
        out_specs=pl.BlockSpec((tm, tn), lambda i, j: (i, j)),
        out_shape=jax.ShapeDtypeStruct((tokens, n), F32),
        compiler_params=_params("parallel", "parallel"), name="out_proj0")(ya_t, yb, w_out, w_out)


def _ln_kernel(h_ref, mix_ref, g_ref, b_ref, o_ref, ob_ref):
    y = DEEPNORM_ALPHA * h_ref[...] + mix_ref[...]
    mu = jnp.mean(y, axis=-1, keepdims=True)
    yc = y - mu
    var = jnp.mean(yc * yc, axis=-1, keepdims=True)
    out = yc * lax.rsqrt(var + LN_EPS) * g_ref[...] + b_ref[...]
    o_ref[...] = out
    ob_ref[...] = out.astype(BF16)


def deepnorm_ln(h, mix, g, b, tm=256):
    tokens, d = h.shape
    tm = _tile(tokens, tm)
    row = pl.BlockSpec((tm, d), lambda i: (i, 0))
    vec = pl.BlockSpec((1, d), lambda i: (0, 0))
    return pl.pallas_call(
        _ln_kernel, grid=(tokens // tm,),
        in_specs=[row, row, vec, vec], out_specs=[row, row],
        out_shape=[jax.ShapeDtypeStruct((tokens, d), F32), jax.ShapeDtypeStruct((tokens, d), BF16)],
        compiler_params=_params("parallel"), name="deepnorm_ln")(h, mix, g.reshape(1, d), b.reshape(1, d))


def _gate_kernel(hb_ref, w_ref, bg_ref, h_ref, p_ref, wp_ref, o_ref, ob_ref):
    gate = jax.nn.sigmoid(jnp.dot(hb_ref[...], w_ref[...], preferred_element_type=F32) + bg_ref[...])
    out = h_ref[...] + gate * jnp.dot(p_ref[...], wp_ref[...], preferred_element_type=F32)
    o_ref[...] = out
    ob_ref[...] = out.astype(BF16)


def ple_gate(h, hb, w_gate, b_gate, p, w_ple, tm=1024, tn=512):
    tokens, d = h.shape
    pd = p.shape[1]
    tm, tn = _tile(tokens, tm), _tile(d, tn)
    return pl.pallas_call(
        _gate_kernel, grid=(tokens // tm, d // tn),
        in_specs=[pl.BlockSpec((tm, d), lambda i, j: (i, 0)),
                  pl.BlockSpec((d, tn), lambda i, j: (0, j)),
                  pl.BlockSpec((1, tn), lambda i, j: (0, j)),
                  pl.BlockSpec((tm, tn), lambda i, j: (i, j)),
                  pl.BlockSpec((tm, pd), lambda i, j: (i, 0)),
                  pl.BlockSpec((pd, tn), lambda i, j: (0, j))],
        out_specs=[pl.BlockSpec((tm, tn), lambda i, j: (i, j))] * 2,
        out_shape=[jax.ShapeDtypeStruct((tokens, d), F32), jax.ShapeDtypeStruct((tokens, d), BF16)],
        compiler_params=_params("parallel", "parallel"), name="ple_gate")(
            hb, w_gate, b_gate.reshape(1, d), h, p, w_ple)


def _distance_tables(dil):
    groups = DEINT // dil
    rows = ATT_GROUP_ROWS[dil]
    f_q, a_q = np.divmod(np.arange(groups * rows), rows)
    iq = groups * a_q + f_q

    def table(k_rows, a0):
        f_k, a_k = np.divmod(np.arange(groups * k_rows), k_rows)
        delta = iq[:, None] - (groups * (a_k + a0) + f_k)[None, :]
        return np.where((delta >= 0) & (delta <= SPAN), dil * delta, FAR).astype(np.float32)

    return table(2 * rows, -rows), table(rows, 0)


def _attn_kernel(sl_ref, *refs):
    nb = len(DILATED)
    dist_refs, unperm_ref = refs[:2 * nb], refs[2 * nb]
    q_ref, k_ref, v_ref, z_ref, o_ref, m_ref, l_ref, acc_ref = refs[2 * nb + 1:2 * nb + 9]
    bias_refs = refs[2 * nb + 9:]
    n_res = q_ref.shape[1]
    slope = sl_ref[0]
    for d_ref, b_ref in zip(dist_refs, bias_refs):
        b_ref[...] = -slope * d_ref[...]

    def rows_of(ref, res, starts, size):
        if len(ref.shape) == 3:
            return jnp.concatenate([ref[r, pl.ds(s, size), :] for r, s in zip(res, starts)], axis=0)
        return jnp.concatenate([ref[pl.ds(r * n_res + s, size), :] for r, s in zip(res, starts)], axis=0)

    def block_group(dil, items, tabs, fresh, final):
        groups, rows = DEINT // dil, ATT_GROUP_ROWS[dil]
        align = min(rows, 16)
        res = [[dil * f + c for f in range(groups)] for c, _, _ in items]
        q_at = [[pl.multiple_of(rows * j, align)] * groups for _, j, _ in items]
        k_at = [qa if first else [pl.multiple_of(rows * (j - 1), align)] * groups
                for qa, (_, j, first) in zip(q_at, items)]
        k_rows = [rows if first else 2 * rows for _, _, first in items]
        bias = [tabs[1] if first else tabs[0] for _, _, first in items]
        n = range(len(items))
        ones = jnp.ones((2 * rows * groups, HEAD_DIM), BF16)
        q = [rows_of(q_ref, res[i], q_at[i], rows) for i in n]
        k = [rows_of(k_ref, res[i], k_at[i], k_rows[i]) for i in n]
        v1 = [jnp.concatenate([rows_of(v_ref, res[i], k_at[i], k_rows[i]), ones[:k_rows[i] * groups]], axis=1)
              for i in n]
        if not fresh:
            m_prev = [rows_of(m_ref, res[i], q_at[i], rows) for i in n]
            l_prev = [rows_of(l_ref, res[i], q_at[i], rows) for i in n]
            acc_prev = [rows_of(acc_ref, res[i], q_at[i], rows) for i in n]
        s = [lax.dot_general(q[i], k[i], (((1,), (1,)), ((), ())), preferred_element_type=F32)
             * (HEAD_DIM ** -0.5) + bias[i][...] for i in n]
        m_new = [jnp.broadcast_to(jnp.max(s[i], axis=-1, keepdims=True), (rows * groups, HEAD_DIM)) for i in n]
        if not fresh:
            m_new = [jnp.maximum(m_prev[i], m_new[i]) for i in n]
        p = [jnp.concatenate([jnp.exp(s[i][:, t:t + HEAD_DIM] - m_new[i])
                              for t in range(0, s[i].shape[1], HEAD_DIM)], axis=1).astype(BF16) for i in n]
        pv = [jnp.dot(p[i], v1[i], preferred_element_type=F32) for i in n]
        acc_new = [pv[i][:, :HEAD_DIM] for i in n]
        l_new = [pv[i][:, HEAD_DIM:] for i in n]
        if not fresh:
            alpha = [jnp.exp(m_prev[i] - m_new[i]) for i in n]
            l_new = [alpha[i] * l_prev[i] + l_new[i] for i in n]
            acc_new = [alpha[i] * acc_prev[i] + acc_new[i] for i in n]
        for i in n:
            if final:
                span = rows * groups
                at = pl.multiple_of(span * items[i][1], span)
                o = jnp.dot(unperm_ref[...], (acc_new[i] / l_new[i]).astype(BF16), preferred_element_type=F32)
                z = z_ref[pl.ds(at, span), :]
                o_ref[pl.ds(at, span), :] = (o * (z * jax.nn.sigmoid(z))).astype(o_ref.dtype)
                continue
            for f in range(groups):
                at = res[i][f] * n_res + q_at[i][f]
                part = slice(f * rows, (f + 1) * rows)
                m_ref[pl.ds(at, rows), :] = m_new[i][part]
                l_ref[pl.ds(at, rows), :] = l_new[i][part]
                acc_ref[pl.ds(at, rows), :] = acc_new[i][part]

    order = sorted(range(nb), key=lambda i: -DILATED[i][1])
    for pos, idx in enumerate(order):
        dil = DILATED[idx][1]
        fresh, final = pos == 0, pos == nb - 1
        tabs = (bias_refs[2 * idx], bias_refs[2 * idx + 1])
        n_blocks = n_res // ATT_GROUP_ROWS[dil]
        per = min(ATT_BLOCKS_PER_STEP, n_blocks)
        subs = min(ATT_BLOCKS_PER_STEP // per, dil)
        assert n_blocks % per == 0 and dil % subs == 0

        def subsequences(cc, carry, dil=dil, fresh=fresh, final=final, tabs=tabs, n_blocks=n_blocks, per=per,
                         subs=subs):
            cs = [cc * subs + i for i in range(subs)]
            block_group(dil, [(c, j, j == 0) for c in cs for j in range(per)], tabs, fresh, final)
            if n_blocks > per:
                def later(g, carry2):
                    block_group(dil, [(c, g * per + j, False) for c in cs for j in range(per)], tabs, fresh, final)
                    return carry2
                lax.fori_loop(1, n_blocks // per, later, 0)
            return carry
        lax.fori_loop(0, dil // subs, subsequences, 0)


def dilated_attention(qkv, z, batch, seq):
    heads, tokens, _ = z.shape
    n_res = seq // DEINT
    slopes = jnp.asarray(np.exp2(-8.0 * np.arange(1, heads + 1) / heads).astype(np.float32)).reshape(heads, 1, 1)
    tables = [jnp.asarray(t) for _, dil in DILATED for t in _distance_tables(dil)]
    assert min(d for _, d in DILATED) == 1
    rows = ATT_GROUP_ROWS[1]
    tok = np.arange(DEINT * rows)
    unperm = np.zeros((DEINT * rows, DEINT * rows), np.float32)
    unperm[tok, (tok % DEINT) * rows + tok // DEINT] = 1.0
    whole = lambda t: pl.BlockSpec(t.shape, lambda b, h: (0, 0))
    grouped = lambda off: pl.BlockSpec((DEINT, None, n_res, HEAD_DIM), lambda b, h: (b, off + h, 0, 0))
    natural = pl.BlockSpec((None, seq, HEAD_DIM), lambda b, h: (h, b, 0))
    return pl.pallas_call(
        _attn_kernel, grid=(batch, heads),
        in_specs=[pl.BlockSpec((1, 1, 1), lambda b, h: (h, 0, 0))] + [whole(t) for t in tables]
        + [whole(unperm), grouped(0), grouped(heads), grouped(2 * heads), natural],
        out_specs=natural,
        out_shape=jax.ShapeDtypeStruct((heads, tokens, HEAD_DIM), BF16),
        scratch_shapes=[pltpu.VMEM((seq, HEAD_DIM), F32)] * 3
        + [pltpu.VMEM(t.shape, F32) for t in tables],
        compiler_params=_params("parallel", "parallel"),
        name="dilated_attn")(slopes, *tables, jnp.asarray(unperm, BF16), qkv, qkv, qkv, z)


def kernel(x, p, ab_w_in, ab_lambda_re, ab_lambda_im, ab_log_step, ab_b_re, ab_b_im, ab_c_re, ab_c_im, ab_d,
           ab_w_glu, ab_b_glu, ab_conv_w, ab_conv_b, ab_w_out, at_w_in, at_w_out, ln_g, ln_b, ple_w,
           ple_gate_w, ple_gate_b):
    batch, seq, d = x.shape
    tokens = batch * seq
    s5w = d // 2
    assert seq % (DEINT * SPAN) == 0 and seq % CHUNK == 0 and s5w % S5_GROUP == 0
    h = x.reshape(tokens, d)
    hb = h.astype(BF16)
    pb = p.reshape(DEPTH, tokens, PLE_DIM).astype(BF16)

    w_in = ab_w_in[0].astype(BF16)
    proj_t = matmul_nt(w_in[:, :2 * s5w].T, hb, F32)
    y_t = s5_core(proj_t[:s5w], ab_log_step[0], ab_lambda_re[0], ab_lambda_im[0], ab_b_re[0], ab_b_im[0],
                  ab_c_re[0], ab_c_im[0], batch)
    ya_t = s5_glu_t(y_t, proj_t, ab_d[0], ab_w_glu[0].T.astype(BF16), ab_b_glu[0])
    yb = proj_short_conv(hb, w_in, 2 * s5w, ab_conv_w[0], ab_conv_b[0], seq)
    mix = out_proj0(ya_t, yb, ab_w_out[0].astype(BF16))
    h, hb = deepnorm_ln(h, mix, ln_g[0], ln_b[0])
    h, hb = ple_gate(h, hb, ple_gate_w[0].astype(BF16), ple_gate_b[0], pb[0], ple_w[0].astype(BF16))

    w_in = at_w_in[0].astype(BF16)
    qkv = matmul_deinterleaved(hb, w_in, 3 * d, BF16, seq)
    z = matmul_heads_out(hb, w_in, 3 * d, d, F32)
    og = dilated_attention(qkv, z, batch, seq)
    mix = matmul_heads_in(og, at_w_out[0].astype(BF16), F32)
    h, hb = deepnorm_ln(h, mix, ln_g[1], ln_b[1])
    h, _ = ple_gate(h, hb, ple_gate_w[1].astype(BF16), ple_gate_b[1], pb[1], ple_w[1].astype(BF16))
    return h.reshape(batch, seq, d)
```

```python
import functools
import math

import jax
import jax.numpy as jnp
import numpy as np
from jax import lax
from jax.experimental import pallas as pl
from jax.experimental.pallas import tpu as pltpu

F32 = jnp.float32
BF16 = jnp.bfloat16

V7X_VMEM_LIMIT_BYTES = 56 * 1024 * 1024
LANES = 128
S5_GROUP = 16
S5_STATE = 64
CHUNK = LANES
CONV_K = 3
HEAD_DIM = 128
PLE_DIM = 256
LN_EPS = 1e-5
DEPTH = 2
DEEPNORM_ALPHA = (2.0 * DEPTH) ** 0.25
DILATED = ((128, 1), (512, 4), (2048, 16))
SPAN = 128
DEINT = max(d for _, d in DILATED)
ATT_GROUP_ROWS = {16: 128, 4: 32, 1: 8}
FAR = 1e32
ATT_BLOCKS_PER_STEP = 8
assert all(w // d == SPAN for w, d in DILATED)


def _params(*sem):
    return pltpu.CompilerParams(dimension_semantics=sem, vmem_limit_bytes=V7X_VMEM_LIMIT_BYTES)


def _tile(n, pref):
    return pref if n % pref == 0 else n


def _mm_nn_kernel(a_ref, b_ref, o_ref):
    o_ref[...] = jnp.dot(a_ref[...], b_ref[...], preferred_element_type=F32).astype(o_ref.dtype)


def _mm_heads_out_kernel(a_ref, b_ref, o_ref):
    acc = jnp.dot(a_ref[...], b_ref[...], preferred_element_type=F32).astype(o_ref.dtype)
    for h in range(o_ref.shape[0]):
        o_ref[h] = acc[:, h * HEAD_DIM:(h + 1) * HEAD_DIM]


def _mm_heads_in_kernel(a_ref, b_ref, o_ref):
    lhs = jnp.concatenate([a_ref[h] for h in range(a_ref.shape[0])], axis=1)
    o_ref[...] = jnp.dot(lhs, b_ref[...], preferred_element_type=F32).astype(o_ref.dtype)


def _mm_nt_kernel(a_ref, b_ref, o_ref):
    o_ref[...] = lax.dot_general(a_ref[...], b_ref[...], (((1,), (1,)), ((), ())),
                                 preferred_element_type=F32).astype(o_ref.dtype)


def matmul_nn(a, b, out_dtype, tm=1024, tn=1024):
    m, k = a.shape
    _, n = b.shape
    tm, tn = _tile(m, tm), _tile(n, tn)
    return pl.pallas_call(
        _mm_nn_kernel, grid=(m // tm, n // tn),
        in_specs=[pl.BlockSpec((tm, k), lambda i, j: (i, 0)),
                  pl.BlockSpec((k, tn), lambda i, j: (0, j))],
        out_specs=pl.BlockSpec((tm, tn), lambda i, j: (i, j)),
        out_shape=jax.ShapeDtypeStruct((m, n), out_dtype),
        compiler_params=_params("parallel", "parallel"), name="mm_nn")(a, b)


DEINT_SPAN = 512


def matmul_heads_out(a, b, col0, n, out_dtype, tm=1024, tn=1024):
    m, k = a.shape
    tm, tn = _tile(m, tm), _tile(n, tn)
    assert col0 % tn == 0
    return pl.pallas_call(
        _mm_heads_out_kernel, grid=(m // tm, n // tn),
        in_specs=[pl.BlockSpec((tm, k), lambda i, j: (i, 0)),
                  pl.BlockSpec((k, tn), lambda i, j: (0, col0 // tn + j))],
        out_specs=pl.BlockSpec((tn // HEAD_DIM, tm, HEAD_DIM), lambda i, j: (j, i, 0)),
        out_shape=jax.ShapeDtypeStruct((n // HEAD_DIM, m, HEAD_DIM), out_dtype),
        compiler_params=_params("parallel", "parallel"), name="mm_heads_out")(a, b)


def matmul_heads_in(a, b, out_dtype, tm=1024, tn=1024):
    heads, m, _ = a.shape
    k, n = b.shape
    tm, tn = _tile(m, tm), _tile(n, tn)
    return pl.pallas_call(
        _mm_heads_in_kernel, grid=(m // tm, n // tn),
        in_specs=[pl.BlockSpec((heads, tm, HEAD_DIM), lambda i, j: (0, i, 0)),
                  pl.BlockSpec((k, tn), lambda i, j: (0, j))],
        out_specs=pl.BlockSpec((tm, tn), lambda i, j: (i, j)),
        out_shape=jax.ShapeDtypeStruct((m, n), out_dtype),
        compiler_params=_params("parallel", "parallel"), name="mm_heads_in")(a, b)


def _mm_deint_kernel(perm_ref, a_ref, b_ref, o_ref, lhs_ref):
    span = perm_ref.shape[0]
    runs = a_ref.shape[0] // span
    rows = span // DEINT

    @pl.when(pl.program_id(1) == 0)
    def _():
        for t in range(runs):
            lhs_ref[t * span:(t + 1) * span, :] = jnp.dot(
                perm_ref[...], a_ref[t * span:(t + 1) * span, :], preferred_element_type=F32).astype(lhs_ref.dtype)

    acc = jnp.dot(lhs_ref[...], b_ref[...], preferred_element_type=F32).astype(o_ref.dtype)
    for t in range(runs):
        for r in range(DEINT):
            for h in range(o_ref.shape[1]):
                o_ref[r, h, t * rows:(t + 1) * rows, :] = (
                    acc[t * span + r * rows:t * span + (r + 1) * rows, h * HEAD_DIM:(h + 1) * HEAD_DIM])


def matmul_deinterleaved(a, b, n, out_dtype, seq, tm=1024, tn=1024):
    m, k = a.shape
    tm, tn = _tile(seq, tm), _tile(n, tn)
    span = min(DEINT_SPAN, tm)
    rows, nt = span // DEINT, seq // tm
    perm = np.zeros((span, span), np.float32)
    tok = np.arange(span)
    perm[(tok % DEINT) * rows + tok // DEINT, tok] = 1.0
    return pl.pallas_call(
        _mm_deint_kernel, grid=(m // tm, n // tn),
        in_specs=[pl.BlockSpec((span, span), lambda i, j: (0, 0)),
                  pl.BlockSpec((tm, k), lambda i, j: (i, 0)),
                  pl.BlockSpec((k, tn), lambda i, j: (0, j))],
        out_specs=pl.BlockSpec((DEINT, tn // HEAD_DIM, tm // DEINT, HEAD_DIM), lambda i, j: (i // nt, j, i % nt, 0)),
        out_shape=jax.ShapeDtypeStruct((m // seq * DEINT, n // HEAD_DIM, seq // DEINT, HEAD_DIM), out_dtype),
        scratch_shapes=[pltpu.VMEM((tm, k), a.dtype)],
        compiler_params=_params("parallel", "arbitrary"), name="mm_deint")(jnp.asarray(perm, a.dtype), a, b)


def _mm_nt_chunked_kernel(a_ref, b_ref, o_ref):
    acc = lax.dot_general(a_ref[...], b_ref[...], (((1,), (1,)), ((), ())), preferred_element_type=F32)
    for c in range(o_ref.shape[1]):
        o_ref[:, c, :] = acc[:, c * LANES:(c + 1) * LANES].astype(o_ref.dtype)


def matmul_nt(a, b, row0, m, out_dtype, chunked=False, tm=1024, tn=1024):
    _, k = a.shape
    n, _ = b.shape
    tm, tn = _tile(m, tm), _tile(n, tn)
    assert row0 % tm == 0
    if chunked:
        body, o_spec = _mm_nt_chunked_kernel, pl.BlockSpec((tm, tn // LANES, LANES), lambda i, j: (i, j, 0))
        o_shape = (m, n // LANES, LANES)
    else:
        body, o_spec, o_shape = _mm_nt_kernel, pl.BlockSpec((tm, tn), lambda i, j: (i, j)), (m, n)
    return pl.pallas_call(
        body, grid=(m // tm, n // tn),
        in_specs=[pl.BlockSpec((tm, k), lambda i, j: (row0 // tm + i, 0)),
                  pl.BlockSpec((tn, k), lambda i, j: (j, 0))],
        out_specs=o_spec,
        out_shape=jax.ShapeDtypeStruct(o_shape, out_dtype),
        compiler_params=_params("parallel", "parallel"), name="mm_nt")(a, b)


S5_PARAMS = 6
S5_PAIR = 2


def _s5_toeplitz_rows(kmat_ref, m_ref, ci):
    causal = (lax.broadcasted_iota(jnp.int32, (CHUNK, CHUNK), 1)
              >= lax.broadcasted_iota(jnp.int32, (CHUNK, CHUNK), 0))
    row0 = pl.multiple_of(ci * CHUNK, CHUNK)
    for co in range(S5_GROUP):
        krow = kmat_ref[pl.ds(ci * S5_GROUP + co, 1), :]
        shifted = pltpu.roll(jnp.broadcast_to(krow, (CHUNK, CHUNK)), 0, 1, stride=1, stride_axis=0)
        m_ref[pl.ds(row0, CHUNK), co * CHUNK:(co + 1) * CHUNK] = jnp.where(causal, shifted, 0.0).astype(BF16)


def _s5_prepare(params, gi, slot, kmat_ref):
    ls_ref, laml_ref, lams_ref, bt_ref, c_ref, ct_ref = params
    m_ref, wst_re_ref, wst_im_ref, wa_ref, wb_ref, ac_ref = slot
    grp = S5_GROUP
    dt = jnp.exp(ls_ref[gi])
    a_re_l = jnp.minimum(laml_ref[gi, 0:1, :], -1e-4)
    a_im_l = laml_ref[gi, 1:2, :]
    a_re_s = jnp.minimum(lams_ref[gi, :, 0:1], -1e-4)
    a_im_s = lams_ref[gi, :, 1:2]

    def a_bar(a_re, a_im):
        mag = jnp.exp(dt * a_re)
        ang = dt * a_im
        return mag * jnp.cos(ang), mag * jnp.sin(ang)

    def cpow(base, n, bits):
        p_re, p_im = base
        re, im = jnp.ones(n.shape, F32), jnp.zeros(n.shape, F32)
        for b in range(bits):
            on = ((n >> b) & 1) == 1
            f_re, f_im = jnp.where(on, p_re, 1.0), jnp.where(on, p_im, 0.0)
            re, im = re * f_re - im * f_im, re * f_im + im * f_re
            p_re, p_im = p_re * p_re - p_im * p_im, 2.0 * p_re * p_im
        return re, im

    bits = CHUNK.bit_length()
    ab_l = a_bar(a_re_l, a_im_l)
    ab_s = a_bar(a_re_s, a_im_s)

    ab_re, ab_im = ab_l
    den = a_re_l * a_re_l + a_im_l * a_im_l
    nr = ab_re - 1.0
    co_re = (nr * a_re_l + ab_im * a_im_l) / den
    co_im = (ab_im * a_re_l - nr * a_im_l) / den
    br_t, bi_t = bt_ref[gi, 0], bt_ref[gi, 1]
    bbr = co_re * br_t - co_im * bi_t
    bbi = co_re * bi_t + co_im * br_t
    cr, ci_ = c_ref[gi, 0], c_ref[gi, 1]

    g1 = jnp.concatenate([bbr[i:i + 1] * cr - bbi[i:i + 1] * ci_ for i in range(grp)], axis=0)
    g2 = jnp.concatenate([bbr[i:i + 1] * ci_ + bbi[i:i + 1] * cr for i in range(grp)], axis=0)
    tau = lax.broadcasted_iota(jnp.int32, (S5_STATE, CHUNK), 1)
    e_re, e_im = cpow(ab_s, tau, bits)
    kmat_ref[...] = (jnp.dot(g1, e_re, precision=lax.Precision.HIGHEST, preferred_element_type=F32)
                     - jnp.dot(g2, e_im, precision=lax.Precision.HIGHEST, preferred_element_type=F32))

    rev = CHUNK - 1 - lax.broadcasted_iota(jnp.int32, (CHUNK, S5_STATE), 0)
    er_re, er_im = cpow(ab_l, rev, bits)
    for i in range(grp):
        wst_re_ref[i * CHUNK:(i + 1) * CHUNK, :] = (er_re * bbr[i:i + 1] - er_im * bbi[i:i + 1]).astype(BF16)
        wst_im_ref[i * CHUNK:(i + 1) * CHUNK, :] = (er_re * bbi[i:i + 1] + er_im * bbr[i:i + 1]).astype(BF16)

    e1_re = e_re * ab_s[0] - e_im * ab_s[1]
    e1_im = e_re * ab_s[1] + e_im * ab_s[0]
    cr_t, ci_t = ct_ref[gi, 0], ct_ref[gi, 1]
    for o in range(grp):
        wa_ref[:, o * CHUNK:(o + 1) * CHUNK] = (cr_t[:, o:o + 1] * e1_re - ci_t[:, o:o + 1] * e1_im).astype(BF16)
        wb_ref[:, o * CHUNK:(o + 1) * CHUNK] = (-(cr_t[:, o:o + 1] * e1_im + ci_t[:, o:o + 1] * e1_re)).astype(BF16)

    ac_re, ac_im = cpow(ab_l, jnp.full((1, S5_STATE), CHUNK, jnp.int32), bits)
    ac_ref[0:1, :] = ac_re
    ac_ref[1:2, :] = ac_im


S5_K_SLICE = 2


def _s5_kernel(*refs, batch):
    cur, nxt = refs[:S5_PARAMS], refs[S5_PARAMS:2 * S5_PARAMS]
    d_ref, u_ref, o_ref, kmat_ref = refs[2 * S5_PARAMS:2 * S5_PARAMS + 4]
    rest = refs[2 * S5_PARAMS + 4:]
    slots = [rest[6 * i:6 * i + 6] for i in range(S5_PAIR)]
    ybig_ref, sre_ref, sim_ref, hre_ref, him_ref = rest[6 * S5_PAIR:]
    grp = S5_GROUP
    nc = u_ref.shape[1]
    nck = nc // batch

    @pl.when(pl.program_id(0) == 0)
    def _():
        _s5_prepare(cur, 0, slots[0], kmat_ref)

        def rows(ci, carry):
            _s5_toeplitz_rows(kmat_ref, slots[0][0], ci)
            return carry
        lax.fori_loop(0, grp, rows, 0)

    for gi in range(S5_PAIR):
        m_ref, wst_re_ref, wst_im_ref, wa_ref, wb_ref, ac_ref = slots[gi]
        following = slots[(gi + 1) % S5_PAIR]
        lhs = jnp.concatenate([u_ref[gi * grp + i].astype(BF16) for i in range(grp)], axis=1)
        sre_ref[...] = jnp.dot(lhs, wst_re_ref[...], preferred_element_type=F32)
        sim_ref[...] = jnp.dot(lhs, wst_im_ref[...], preferred_element_type=F32)
        if gi + 1 < S5_PAIR:
            _s5_prepare(cur, gi + 1, following, kmat_ref)
        else:
            _s5_prepare(nxt, 0, following, kmat_ref)
        ybig_ref[...] = jnp.zeros(ybig_ref.shape, F32)

        def contract(it, carry, gi=gi, m_ref=m_ref, following=following):
            chans = [gi * grp + it * S5_K_SLICE + j for j in range(S5_K_SLICE)]
            part = jnp.concatenate([u_ref[ch].astype(BF16) for ch in chans], axis=1)
            k0 = pl.multiple_of(it * (S5_K_SLICE * CHUNK), S5_K_SLICE * CHUNK)
            ybig_ref[...] += jnp.dot(part, m_ref[pl.ds(k0, S5_K_SLICE * CHUNK), :], preferred_element_type=F32)
            for j in range(S5_K_SLICE):
                _s5_toeplitz_rows(kmat_ref, following[0], it * S5_K_SLICE + j)
            return carry
        lax.fori_loop(0, grp // S5_K_SLICE, contract, 0)

        ac_re, ac_im = ac_ref[0:1, :], ac_ref[1:2, :]

        def carry_state(k, carry, ac_re=ac_re, ac_im=ac_im):
            new = []
            for b in range(batch):
                h_re, h_im = carry[2 * b], carry[2 * b + 1]
                row = b * nck + k
                hre_ref[pl.ds(row, 1), :] = h_re
                him_ref[pl.ds(row, 1), :] = h_im
                new.append(ac_re * h_re - ac_im * h_im + sre_ref[pl.ds(row, 1), :])
                new.append(ac_re * h_im + ac_im * h_re + sim_ref[pl.ds(row, 1), :])
            return tuple(new)

        zero = jnp.zeros((1, S5_STATE), F32)
        lax.fori_loop(0, nck, carry_state, (zero,) * (2 * batch))

        y = (ybig_ref[...]
             + jnp.dot(hre_ref[...].astype(BF16), wa_ref[...], preferred_element_type=F32)
             + jnp.dot(him_ref[...].astype(BF16), wb_ref[...], preferred_element_type=F32))
        for o in range(grp):
            ch = gi * grp + o
            o_ref[ch] = jax.nn.gelu(y[:, o * CHUNK:(o + 1) * CHUNK] + d_ref[ch] * u_ref[ch])


def s5_core(u3, d_skip, log_step, lam_re, lam_im, b_re, b_im, c_re, c_im, batch):
    width, nc, _ = u3.shape
    tokens = nc * CHUNK
    groups = width // S5_GROUP
    assert groups % S5_PAIR == 0
    steps = groups // S5_PAIR
    ls = log_step.reshape(groups, 1, 1)
    lam_l = jnp.stack([lam_re, lam_im], axis=1)
    lam_s = jnp.stack([lam_re, lam_im], axis=2)
    b_t = jnp.stack([b_re, b_im], axis=1).transpose(0, 1, 3, 2)
    c = jnp.stack([c_re, c_im], axis=1)
    c_t = c.transpose(0, 1, 3, 2)
    params = (ls, lam_l, lam_s, b_t, c, c_t)
    gw = S5_GROUP * CHUNK
    cur = lambda a: pl.BlockSpec((S5_PAIR,) + a.shape[1:], lambda g: (g,) + (0,) * (a.ndim - 1))
    nxt = lambda a: pl.BlockSpec((1,) + a.shape[1:],
                                 lambda g: (jnp.minimum(S5_PAIR * (g + 1), groups - 1),) + (0,) * (a.ndim - 1))
    chan = pl.BlockSpec((S5_PAIR * S5_GROUP, nc, CHUNK), lambda g: (g, 0, 0))
    slot = [pltpu.VMEM((gw, gw), BF16),
            pltpu.VMEM((gw, S5_STATE), BF16), pltpu.VMEM((gw, S5_STATE), BF16),
            pltpu.VMEM((S5_STATE, gw), BF16), pltpu.VMEM((S5_STATE, gw), BF16),
            pltpu.VMEM((2, S5_STATE), F32)]
    out = pl.pallas_call(
        functools.partial(_s5_kernel, batch=batch), grid=(steps,),
        in_specs=[cur(a) for a in params] + [nxt(a) for a in params]
        + [pl.BlockSpec((S5_PAIR * S5_GROUP, 1, 1), lambda g: (g, 0, 0)), chan],
        out_specs=chan,
        out_shape=jax.ShapeDtypeStruct((width, nc, CHUNK), F32),
        scratch_shapes=[pltpu.VMEM((S5_GROUP * S5_GROUP, CHUNK), F32)] + slot * S5_PAIR
        + [pltpu.VMEM((nc, gw), F32)] + [pltpu.VMEM((nc, S5_STATE), F32)] * 4,
        compiler_params=_params("arbitrary"), name="s5_core")(*params, *params, d_skip.reshape(width, 1, 1), u3)
    return out.reshape(width, tokens)


def _glu_t_kernel(y_ref, z_ref, w_ref, b_ref, o_ref):
    y = y_ref[...]
    acc = jnp.dot(w_ref[...], y.astype(BF16), preferred_element_type=F32)
    z = z_ref[...]
    o_ref[...] = (y * jax.nn.sigmoid(acc + b_ref[...]) * (z * jax.nn.sigmoid(z))).astype(o_ref.dtype)


def s5_glu_t(y_t, z_t, w_glu_t, b_glu, tm=256):
    width, tokens = y_t.shape
    tm = _tile(tokens, tm)
    col = lambda i: (0, i)
    return pl.pallas_call(
        _glu_t_kernel, grid=(tokens // tm,),
        in_specs=[pl.BlockSpec((width, tm), col), pl.BlockSpec((width, tm), col),
                  pl.BlockSpec((width, width), lambda i: (0, 0)),
                  pl.BlockSpec((width, 1), lambda i: (0, 0))],
        out_specs=pl.BlockSpec((width, tm), col),
        out_shape=jax.ShapeDtypeStruct((width, tokens), BF16),
        compiler_params=_params("parallel"), name="s5_glu")(
            y_t, z_t, w_glu_t, b_glu.reshape(width, 1))


def _proj_conv_kernel(x_ref, wh_ref, wg_ref, wc_ref, wz_ref, cw_ref, cb_ref, o_ref, tail_ref, *, seq):
    tm = x_ref.shape[0]
    x = x_ref[...]
    v = (jnp.dot(x, wc_ref[...], preferred_element_type=F32) * jnp.dot(x, wh_ref[...], preferred_element_type=F32))
    first = (pl.program_id(1) * tm) % seq == 0
    ext = jnp.concatenate([jnp.where(first, 0.0, tail_ref[...]), v], axis=0)
    conv = (cw_ref[0:1, :] * ext[6:6 + tm] + cw_ref[1:2, :] * ext[7:7 + tm] + cw_ref[2:3, :] * v
            + cb_ref[...])
    tail_ref[...] = v[tm - 8:]
    z = jnp.dot(x, wz_ref[...], preferred_element_type=F32)
    o_ref[...] = (jnp.dot(x, wg_ref[...], preferred_element_type=F32) * conv
                  * (z * jax.nn.sigmoid(z))).astype(o_ref.dtype)


def proj_short_conv(x, w_in, col0, conv_w, conv_b, seq, tm=1024, tc=256):
    tokens, k = x.shape
    width = conv_w.shape[1]
    tm, tc = _tile(seq, tm), _tile(width, tc)
    ncb = width // tc
    seg = lambda s: pl.BlockSpec((k, tc), lambda j, i: (0, (col0 + s * width) // tc + j))
    return pl.pallas_call(
        functools.partial(_proj_conv_kernel, seq=seq), grid=(ncb, tokens // tm),
        in_specs=[pl.BlockSpec((tm, k), lambda j, i: (i, 0)), seg(0), seg(1), seg(2), seg(3),
                  pl.BlockSpec((CONV_K, tc), lambda j, i: (0, j)),
                  pl.BlockSpec((1, tc), lambda j, i: (0, j))],
        out_specs=pl.BlockSpec((tm, tc), lambda j, i: (i, j)),
        out_shape=jax.ShapeDtypeStruct((tokens, width), BF16),
        scratch_shapes=[pltpu.VMEM((8, tc), F32)],
        compiler_params=_params("parallel", "arbitrary"), name="proj_short_conv")(
            x, w_in, w_in, w_in, w_in, conv_w, conv_b.reshape(1, width))


def _out0_kernel(ya_ref, yb_ref, wa_ref, wb_ref, o_ref):
    acc = lax.dot_general(ya_ref[...], wa_ref[...], (((0,), (0,)), ((), ())), preferred_element_type=F32)
    o_ref[...] = acc + jnp.dot(yb_ref[...], wb_ref[...], preferred_element_type=F32)


def out_proj0(ya_t, yb, w_out, tm=1024, tn=1024):
    wa_rows, tokens = ya_t.shape
    wb_rows = yb.shape[1]
    n = w_out.shape[1]
    tm, tn = _tile(tokens, tm), _tile(n, tn)
    return pl.pallas_call(
        _out0_kernel, grid=(tokens // tm, n // tn),
        in_specs=[pl.BlockSpec((wa_rows, tm), lambda i, j: (0, i)),
                  pl.BlockSpec((tm, wb_rows), lambda i, j: (i, 0)),
                  pl.BlockSpec((wa_rows, tn), lambda i, j: (0, j)),
                  pl.BlockSpec((wb_rows, tn), lambda i, j: (wa_rows // wb_rows, j))],
        out_specs=pl.BlockSpec((tm, tn), lambda i, j: (i, j)),
        out_shape=jax.ShapeDtypeStruct((tokens, n), F32),
        compiler_params=_params("parallel", "parallel"), name="out_proj0")(ya_t, yb, w_out, w_out)


def _ln_kernel(h_ref, mix_ref, g_ref, b_ref, o_ref, ob_ref):
    y = DEEPNORM_ALPHA * h_ref[...] + mix_ref[...]
    mu = jnp.mean(y, axis=-1, keepdims=True)
    yc = y - mu
    var = jnp.mean(yc * yc, axis=-1, keepdims=True)
    out = yc * lax.rsqrt(var + LN_EPS) * g_ref[...] + b_ref[...]
    o_ref[...] = out
    ob_ref[...] = out.astype(BF16)


def deepnorm_ln(h, mix, g, b, tm=256):
    tokens, d = h.shape
    tm = _tile(tokens, tm)
    row = pl.BlockSpec((tm, d), lambda i: (i, 0))
    vec = pl.BlockSpec((1, d), lambda i: (0, 0))
    return pl.pallas_call(
        _ln_kernel, grid=(tokens // tm,),
        in_specs=[row, row, vec, vec], out_specs=[row, row],
        out_shape=[jax.ShapeDtypeStruct((tokens, d), F32), jax.ShapeDtypeStruct((tokens, d), BF16)],
        compiler_params=_params("parallel"), name="deepnorm_ln")(h, mix, g.reshape(1, d), b.reshape(1, d))


EPILOGUE_SPLIT = 4


def _gate_kernel(hb_ref, w_ref, bg_ref, h_ref, p_ref, wp_ref, o_ref, ob_ref):
    slab = hb_ref.shape[0] // EPILOGUE_SPLIT
    for s in range(EPILOGUE_SPLIT):
        rows = slice(s * slab, (s + 1) * slab)
        gate = jax.nn.sigmoid(jnp.dot(hb_ref[rows, :], w_ref[...], preferred_element_type=F32) + bg_ref[...])
        out = h_ref[rows, :] + gate * jnp.dot(p_ref[rows, :], wp_ref[...], preferred_element_type=F32)
        o_ref[rows, :] = out
        ob_ref[rows, :] = out.astype(BF16)


def ple_gate(h, hb, w_gate, b_gate, p, w_ple, tm=1024, tn=512):
    tokens, d = h.shape
    pd = p.shape[1]
    tm, tn = _tile(tokens, tm), _tile(d, tn)
    return pl.pallas_call(
        _gate_kernel, grid=(tokens // tm, d // tn),
        in_specs=[pl.BlockSpec((tm, d), lambda i, j: (i, 0)),
                  pl.BlockSpec((d, tn), lambda i, j: (0, j)),
                  pl.BlockSpec((1, tn), lambda i, j: (0, j)),
                  pl.BlockSpec((tm, tn), lambda i, j: (i, j)),
                  pl.BlockSpec((tm, pd), lambda i, j: (i, 0)),
                  pl.BlockSpec((pd, tn), lambda i, j: (0, j))],
        out_specs=[pl.BlockSpec((tm, tn), lambda i, j: (i, j))] * 2,
        out_shape=[jax.ShapeDtypeStruct((tokens, d), F32), jax.ShapeDtypeStruct((tokens, d), BF16)],
        compiler_params=_params("parallel", "parallel"), name="ple_gate")(
            hb, w_gate, b_gate.reshape(1, d), h, p, w_ple)


def _distance_tables(dil):
    groups = DEINT // dil
    rows = ATT_GROUP_ROWS[dil]
    f_q, a_q = np.divmod(np.arange(groups * rows), rows)
    iq = groups * a_q + f_q

    def table(k_rows, a0):
        f_k, a_k = np.divmod(np.arange(groups * k_rows), k_rows)
        delta = iq[:, None] - (groups * (a_k + a0) + f_k)[None, :]
        return np.where((delta >= 0) & (delta <= SPAN), dil * delta, FAR).astype(np.float32)

    return table(2 * rows, -rows), table(rows, 0)


def _attn_kernel(sl_ref, *refs):
    nb = len(DILATED)
    dist_refs, unperm_ref = refs[:2 * nb], refs[2 * nb]
    q_ref, k_ref, v_ref, z_ref, o_ref, m_ref, l_ref, acc_ref = refs[2 * nb + 1:2 * nb + 9]
    bias_refs = refs[2 * nb + 9:]
    n_res = q_ref.shape[1]
    slope = sl_ref[0]
    for d_ref, b_ref in zip(dist_refs, bias_refs):
        b_ref[...] = -slope * d_ref[...]

    def rows_of(ref, res, starts, size):
        if len(ref.shape) == 3:
            return jnp.concatenate([ref[r, pl.ds(s, size), :] for r, s in zip(res, starts)], axis=0)
        return jnp.concatenate([ref[pl.ds(r * n_res + s, size), :] for r, s in zip(res, starts)], axis=0)

    def block_group(dil, items, tabs, fresh, final):
        groups, rows = DEINT // dil, ATT_GROUP_ROWS[dil]
        align = min(rows, 16)
        res = [[dil * f + c for f in range(groups)] for c, _, _ in items]
        q_at = [[pl.multiple_of(rows * j, align)] * groups for _, j, _ in items]
        k_at = [qa if first else [pl.multiple_of(rows * (j - 1), align)] * groups
                for qa, (_, j, first) in zip(q_at, items)]
        k_rows = [rows if first else 2 * rows for _, _, first in items]
        bias = [tabs[1] if first else tabs[0] for _, _, first in items]
        n = range(len(items))
        ones = jnp.ones((2 * rows * groups, HEAD_DIM), BF16)
        q = [rows_of(q_ref, res[i], q_at[i], rows) for i in n]
        k = [rows_of(k_ref, res[i], k_at[i], k_rows[i]) for i in n]
        v1 = [jnp.concatenate([rows_of(v_ref, res[i], k_at[i], k_rows[i]), ones[:k_rows[i] * groups]], axis=1)
              for i in n]
        if not fresh:
            m_prev = [rows_of(m_ref, res[i], q_at[i], rows) for i in n]
            l_prev = [rows_of(l_ref, res[i], q_at[i], rows) for i in n]
            acc_prev = [rows_of(acc_ref, res[i], q_at[i], rows) for i in n]
        s = [lax.dot_general(q[i], k[i], (((1,), (1,)), ((), ())), preferred_element_type=F32)
             * (HEAD_DIM ** -0.5) + bias[i][...] for i in n]
        m_new = [jnp.broadcast_to(jnp.max(s[i], axis=-1, keepdims=True), (rows * groups, HEAD_DIM)) for i in n]
        if not fresh:
            m_new = [jnp.maximum(m_prev[i], m_new[i]) for i in n]
        p = [jnp.concatenate([jnp.exp(s[i][:, t:t + HEAD_DIM] - m_new[i])
                              for t in range(0, s[i].shape[1], HEAD_DIM)], axis=1).astype(BF16) for i in n]
        pv = [jnp.dot(p[i], v1[i], preferred_element_type=F32) for i in n]
        acc_new = [pv[i][:, :HEAD_DIM] for i in n]
        l_new = [pv[i][:, HEAD_DIM:] for i in n]
        if not fresh:
            alpha = [jnp.exp(m_prev[i] - m_new[i]) for i in n]
            l_new = [alpha[i] * l_prev[i] + l_new[i] for i in n]
            acc_new = [alpha[i] * acc_prev[i] + acc_new[i] for i in n]
        for i in n:
            if final:
                span = rows * groups
                at = pl.multiple_of(span * items[i][1], span)
                o = jnp.dot(unperm_ref[...], (acc_new[i] / l_new[i]).astype(BF16), preferred_element_type=F32)
                z = z_ref[pl.ds(at, span), :]
                o_ref[pl.ds(at, span), :] = (o * (z * jax.nn.sigmoid(z))).astype(o_ref.dtype)
                continue
            for f in range(groups):
                at = res[i][f] * n_res + q_at[i][f]
                part = slice(f * rows, (f + 1) * rows)
                m_ref[pl.ds(at, rows), :] = m_new[i][part]
                l_ref[pl.ds(at, rows), :] = l_new[i][part]
                acc_ref[pl.ds(at, rows), :] = acc_new[i][part]

    order = sorted(range(nb), key=lambda i: -DILATED[i][1])
    for pos, idx in enumerate(order):
        dil = DILATED[idx][1]
        fresh, final = pos == 0, pos == nb - 1
        tabs = (bias_refs[2 * idx], bias_refs[2 * idx + 1])
        n_blocks = n_res // ATT_GROUP_ROWS[dil]
        per = min(ATT_BLOCKS_PER_STEP, n_blocks)
        subs = min(ATT_BLOCKS_PER_STEP // per, dil)
        assert n_blocks % per == 0 and dil % subs == 0

        def subsequences(cc, carry, dil=dil, fresh=fresh, final=final, tabs=tabs, n_blocks=n_blocks, per=per,
                         subs=subs):
            cs = [cc * subs + i for i in range(subs)]
            block_group(dil, [(c, j, j == 0) for c in cs for j in range(per)], tabs, fresh, final)
            if n_blocks > per:
                def later(g, carry2):
                    block_group(dil, [(c, g * per + j, False) for c in cs for j in range(per)], tabs, fresh, final)
                    return carry2
                lax.fori_loop(1, n_blocks // per, later, 0)
            return carry
        lax.fori_loop(0, dil // subs, subsequences, 0)


def dilated_attention(qkv, z, batch, seq):
    heads, tokens, _ = z.shape
    n_res = seq // DEINT
    slopes = jnp.asarray(np.exp2(-8.0 * np.arange(1, heads + 1) / heads).astype(np.float32)).reshape(heads, 1, 1)
    tables = [jnp.asarray(t) for _, dil in DILATED for t in _distance_tables(dil)]
    assert min(d for _, d in DILATED) == 1
    rows = ATT_GROUP_ROWS[1]
    tok = np.arange(DEINT * rows)
    unperm = np.zeros((DEINT * rows, DEINT * rows), np.float32)
    unperm[tok, (tok % DEINT) * rows + tok // DEINT] = 1.0
    whole = lambda t: pl.BlockSpec(t.shape, lambda b, h: (0, 0))
    grouped = lambda off: pl.BlockSpec((DEINT, None, n_res, HEAD_DIM), lambda b, h: (b, off + h, 0, 0))
    natural = pl.BlockSpec((None, seq, HEAD_DIM), lambda b, h: (h, b, 0))
    return pl.pallas_call(
        _attn_kernel, grid=(batch, heads),
        in_specs=[pl.BlockSpec((1, 1, 1), lambda b, h: (h, 0, 0))] + [whole(t) for t in tables]
        + [whole(unperm), grouped(0), grouped(heads), grouped(2 * heads), natural],
        out_specs=natural,
        out_shape=jax.ShapeDtypeStruct((heads, tokens, HEAD_DIM), BF16),
        scratch_shapes=[pltpu.VMEM((seq, HEAD_DIM), F32)] * 3
        + [pltpu.VMEM(t.shape, F32) for t in tables],
        compiler_params=_params("parallel", "parallel"),
        name="dilated_attn")(slopes, *tables, jnp.asarray(unperm, BF16), qkv, qkv, qkv, z)


def kernel(x, p, ab_w_in, ab_lambda_re, ab_lambda_im, ab_log_step, ab_b_re, ab_b_im, ab_c_re, ab_c_im, ab_d,
           ab_w_glu, ab_b_glu, ab_conv_w, ab_conv_b, ab_w_out, at_w_in, at_w_out, ln_g, ln_b, ple_w,
           ple_gate_w, ple_gate_b):
    batch, seq, d = x.shape
    tokens = batch * seq
    s5w = d // 2
    assert seq % (DEINT * SPAN) == 0 and seq % CHUNK == 0 and s5w % S5_GROUP == 0
    h = x.reshape(tokens, d)
    hb = h.astype(BF16)
    pb = p.reshape(DEPTH, tokens, PLE_DIM).astype(BF16)

    w_in = ab_w_in[0].astype(BF16)
    w_t = w_in[:, :2 * s5w].T
    u3 = matmul_nt(w_t, hb, 0, s5w, F32, chunked=True)
    za_t = matmul_nt(w_t, hb, s5w, s5w, F32)
    y_t = s5_core(u3, ab_d[0], ab_log_step[0], ab_lambda_re[0], ab_lambda_im[0], ab_b_re[0], ab_b_im[0],
                  ab_c_re[0], ab_c_im[0], batch)
    ya_t = s5_glu_t(y_t, za_t, ab_w_glu[0].T.astype(BF16), ab_b_glu[0])
    yb = proj_short_conv(hb, w_in, 2 * s5w, ab_conv_w[0], ab_conv_b[0], seq)
    mix = out_proj0(ya_t, yb, ab_w_out[0].astype(BF16))
    h, hb = deepnorm_ln(h, mix, ln_g[0], ln_b[0])
    h, hb = ple_gate(h, hb, ple_gate_w[0].astype(BF16), ple_gate_b[0], pb[0], ple_w[0].astype(BF16))

    w_in = at_w_in[0].astype(BF16)
    qkv = matmul_deinterleaved(hb, w_in, 3 * d, BF16, seq)
    z = matmul_heads_out(hb, w_in, 3 * d, d, F32)
    og = dilated_attention(qkv, z, batch, seq)
    mix = matmul_heads_in(og, at_w_out[0].astype(BF16), F32)
    h, hb = deepnorm_ln(h, mix, ln_g[1], ln_b[1])
    h, _ = ple_gate(h, hb, ple_gate_w[1].astype(BF16), ple_gate_b[1], pb[1], ple_w[1].astype(BF16))
    return h.reshape(batch, seq, d)
```

```python
import functools
import math

import jax
import jax.numpy as jnp
import numpy as np
from jax import lax
from jax.experimental import pallas as pl
from jax.experimental.pallas import tpu as pltpu

F32 = jnp.float32
BF16 = jnp.bfloat16

V7X_VMEM_LIMIT_BYTES = 56 * 1024 * 1024
LANES = 128
S5_GROUP = 16
S5_STATE = 64
CHUNK = LANES
CONV_K = 3
HEAD_DIM = 128
PLE_DIM = 256
LN_EPS = 1e-5
DEPTH = 2
DEEPNORM_ALPHA = (2.0 * DEPTH) ** 0.25
DILATED = ((128, 1), (512, 4), (2048, 16))
SPAN = 128
DEINT = max(d for _, d in DILATED)
ATT_GROUP_ROWS = {16: 128, 4: 32, 1: 8}
FAR = 1e32
ATT_BLOCKS_PER_STEP = {16: 16, 4: 8, 1: 8}
assert all(w // d == SPAN for w, d in DILATED)


def _params(*sem):
    return pltpu.CompilerParams(dimension_semantics=sem, vmem_limit_bytes=V7X_VMEM_LIMIT_BYTES)


def _tile(n, pref):
    return pref if n % pref == 0 else n


def _mm_nn_kernel(a_ref, b_ref, o_ref):
    o_ref[...] = jnp.dot(a_ref[...], b_ref[...], preferred_element_type=F32).astype(o_ref.dtype)


def _mm_heads_out_kernel(a_ref, b_ref, o_ref):
    acc = jnp.dot(a_ref[...], b_ref[...], preferred_element_type=F32).astype(o_ref.dtype)
    for h in range(o_ref.shape[0]):
        o_ref[h] = acc[:, h * HEAD_DIM:(h + 1) * HEAD_DIM]


def _mm_heads_in_kernel(a_ref, b_ref, o_ref):
    lhs = jnp.concatenate([a_ref[h] for h in range(a_ref.shape[0])], axis=1)
    o_ref[...] = jnp.dot(lhs, b_ref[...], preferred_element_type=F32).astype(o_ref.dtype)


def _mm_nt_kernel(a_ref, b_ref, o_ref):
    o_ref[...] = lax.dot_general(a_ref[...], b_ref[...], (((1,), (1,)), ((), ())),
                                 preferred_element_type=F32).astype(o_ref.dtype)


def matmul_nn(a, b, out_dtype, tm=1024, tn=1024):
    m, k = a.shape
    _, n = b.shape
    tm, tn = _tile(m, tm), _tile(n, tn)
    return pl.pallas_call(
        _mm_nn_kernel, grid=(m // tm, n // tn),
        in_specs=[pl.BlockSpec((tm, k), lambda i, j: (i, 0)),
                  pl.BlockSpec((k, tn), lambda i, j: (0, j))],
        out_specs=pl.BlockSpec((tm, tn), lambda i, j: (i, j)),
        out_shape=jax.ShapeDtypeStruct((m, n), out_dtype),
        compiler_params=_params("parallel", "parallel"), name="mm_nn")(a, b)


DEINT_SPAN = 512


def matmul_heads_out(a, b, col0, n, out_dtype, tm=1024, tn=1024):
    m, k = a.shape
    tm, tn = _tile(m, tm), _tile(n, tn)
    assert col0 % tn == 0
    return pl.pallas_call(
        _mm_heads_out_kernel, grid=(m // tm, n // tn),
        in_specs=[pl.BlockSpec((tm, k), lambda i, j: (i, 0)),
                  pl.BlockSpec((k, tn), lambda i, j: (0, col0 // tn + j))],
        out_specs=pl.BlockSpec((tn // HEAD_DIM, tm, HEAD_DIM), lambda i, j: (j, i, 0)),
        out_shape=jax.ShapeDtypeStruct((n // HEAD_DIM, m, HEAD_DIM), out_dtype),
        compiler_params=_params("parallel", "parallel"), name="mm_heads_out")(a, b)


def matmul_heads_in(a, b, out_dtype, tm=1024, tn=1024):
    heads, m, _ = a.shape
    k, n = b.shape
    tm, tn = _tile(m, tm), _tile(n, tn)
    return pl.pallas_call(
        _mm_heads_in_kernel, grid=(m // tm, n // tn),
        in_specs=[pl.BlockSpec((heads, tm, HEAD_DIM), lambda i, j: (0, i, 0)),
                  pl.BlockSpec((k, tn), lambda i, j: (0, j))],
        out_specs=pl.BlockSpec((tm, tn), lambda i, j: (i, j)),
        out_shape=jax.ShapeDtypeStruct((m, n), out_dtype),
        compiler_params=_params("parallel", "parallel"), name="mm_heads_in")(a, b)


def _mm_deint_kernel(perm_ref, a_ref, b_ref, o_ref, lhs_ref):
    span = perm_ref.shape[0]
    runs = a_ref.shape[0] // span
    rows = span // DEINT

    @pl.when(pl.program_id(1) == 0)
    def _():
        for t in range(runs):
            lhs_ref[t * span:(t + 1) * span, :] = jnp.dot(
                perm_ref[...], a_ref[t * span:(t + 1) * span, :], preferred_element_type=F32).astype(lhs_ref.dtype)

    acc = jnp.dot(lhs_ref[...], b_ref[...], preferred_element_type=F32).astype(o_ref.dtype)
    for t in range(runs):
        for r in range(DEINT):
            for h in range(o_ref.shape[1]):
                o_ref[r, h, t * rows:(t + 1) * rows, :] = (
                    acc[t * span + r * rows:t * span + (r + 1) * rows, h * HEAD_DIM:(h + 1) * HEAD_DIM])


def matmul_deinterleaved(a, b, n, out_dtype, seq, tm=1024, tn=1024):
    m, k = a.shape
    tm, tn = _tile(seq, tm), _tile(n, tn)
    span = min(DEINT_SPAN, tm)
    rows, nt = span // DEINT, seq // tm
    perm = np.zeros((span, span), np.float32)
    tok = np.arange(span)
    perm[(tok % DEINT) * rows + tok // DEINT, tok] = 1.0
    return pl.pallas_call(
        _mm_deint_kernel, grid=(m // tm, n // tn),
        in_specs=[pl.BlockSpec((span, span), lambda i, j: (0, 0)),
                  pl.BlockSpec((tm, k), lambda i, j: (i, 0)),
                  pl.BlockSpec((k, tn), lambda i, j: (0, j))],
        out_specs=pl.BlockSpec((DEINT, tn // HEAD_DIM, tm // DEINT, HEAD_DIM), lambda i, j: (i // nt, j, i % nt, 0)),
        out_shape=jax.ShapeDtypeStruct((m // seq * DEINT, n // HEAD_DIM, seq // DEINT, HEAD_DIM), out_dtype),
        scratch_shapes=[pltpu.VMEM((tm, k), a.dtype)],
        compiler_params=_params("parallel", "arbitrary"), name="mm_deint")(jnp.asarray(perm, a.dtype), a, b)


def _mm_nt_chunked_kernel(a_ref, b_ref, o_ref):
    acc = lax.dot_general(a_ref[...], b_ref[...], (((1,), (1,)), ((), ())), preferred_element_type=F32)
    for c in range(o_ref.shape[1]):
        o_ref[:, c, :] = acc[:, c * LANES:(c + 1) * LANES].astype(o_ref.dtype)


def matmul_nt(a, b, row0, m, out_dtype, chunked=False, tm=1024, tn=1024):
    _, k = a.shape
    n, _ = b.shape
    tm, tn = _tile(m, tm), _tile(n, tn)
    assert row0 % tm == 0
    if chunked:
        body, o_spec = _mm_nt_chunked_kernel, pl.BlockSpec((tm, tn // LANES, LANES), lambda i, j: (i, j, 0))
        o_shape = (m, n // LANES, LANES)
    else:
        body, o_spec, o_shape = _mm_nt_kernel, pl.BlockSpec((tm, tn), lambda i, j: (i, j)), (m, n)
    return pl.pallas_call(
        body, grid=(m // tm, n // tn),
        in_specs=[pl.BlockSpec((tm, k), lambda i, j: (row0 // tm + i, 0)),
                  pl.BlockSpec((tn, k), lambda i, j: (j, 0))],
        out_specs=o_spec,
        out_shape=jax.ShapeDtypeStruct(o_shape, out_dtype),
        compiler_params=_params("parallel", "parallel"), name="mm_nt")(a, b)


S5_PARAMS = 6
S5_PAIR = 2


def _s5_toeplitz_rows(kmat_ref, m_ref, ci):
    causal = (lax.broadcasted_iota(jnp.int32, (CHUNK, CHUNK), 1)
              >= lax.broadcasted_iota(jnp.int32, (CHUNK, CHUNK), 0))
    row0 = pl.multiple_of(ci * CHUNK, CHUNK)
    for co in range(S5_GROUP):
        krow = kmat_ref[pl.ds(ci * S5_GROUP + co, 1), :]
        shifted = pltpu.roll(jnp.broadcast_to(krow, (CHUNK, CHUNK)), 0, 1, stride=1, stride_axis=0)
        m_ref[pl.ds(row0, CHUNK), co * CHUNK:(co + 1) * CHUNK] = jnp.where(causal, shifted, 0.0).astype(BF16)


def _s5_prepare(params, gi, slot, kmat_ref):
    ls_ref, laml_ref, lams_ref, bt_ref, c_ref, ct_ref = params
    m_ref, wst_re_ref, wst_im_ref, wa_ref, wb_ref, ac_ref = slot
    grp = S5_GROUP
    dt = jnp.exp(ls_ref[gi])
    a_re_l = jnp.minimum(laml_ref[gi, 0:1, :], -1e-4)
    a_im_l = laml_ref[gi, 1:2, :]
    a_re_s = jnp.minimum(lams_ref[gi, :, 0:1], -1e-4)
    a_im_s = lams_ref[gi, :, 1:2]

    def a_bar(a_re, a_im):
        mag = jnp.exp(dt * a_re)
        ang = dt * a_im
        return mag * jnp.cos(ang), mag * jnp.sin(ang)

    def cpow(base, n, bits):
        p_re, p_im = base
        re, im = jnp.ones(n.shape, F32), jnp.zeros(n.shape, F32)
        for b in range(bits):
            on = ((n >> b) & 1) == 1
            f_re, f_im = jnp.where(on, p_re, 1.0), jnp.where(on, p_im, 0.0)
            re, im = re * f_re - im * f_im, re * f_im + im * f_re
            p_re, p_im = p_re * p_re - p_im * p_im, 2.0 * p_re * p_im
        return re, im

    bits = CHUNK.bit_length()
    ab_l = a_bar(a_re_l, a_im_l)
    ab_s = a_bar(a_re_s, a_im_s)

    ab_re, ab_im = ab_l
    den = a_re_l * a_re_l + a_im_l * a_im_l
    nr = ab_re - 1.0
    co_re = (nr * a_re_l + ab_im * a_im_l) / den
    co_im = (ab_im * a_re_l - nr * a_im_l) / den
    br_t, bi_t = bt_ref[gi, 0], bt_ref[gi, 1]
    bbr = co_re * br_t - co_im * bi_t
    bbi = co_re * bi_t + co_im * br_t
    cr, ci_ = c_ref[gi, 0], c_ref[gi, 1]

    g1 = jnp.concatenate([bbr[i:i + 1] * cr - bbi[i:i + 1] * ci_ for i in range(grp)], axis=0)
    g2 = jnp.concatenate([bbr[i:i + 1] * ci_ + bbi[i:i + 1] * cr for i in range(grp)], axis=0)
    tau = lax.broadcasted_iota(jnp.int32, (S5_STATE, CHUNK), 1)
    e_re, e_im = cpow(ab_s, tau, bits)
    kmat_ref[...] = (jnp.dot(g1, e_re, precision=lax.Precision.HIGHEST, preferred_element_type=F32)
                     - jnp.dot(g2, e_im, precision=lax.Precision.HIGHEST, preferred_element_type=F32))

    rev = CHUNK - 1 - lax.broadcasted_iota(jnp.int32, (CHUNK, S5_STATE), 0)
    er_re, er_im = cpow(ab_l, rev, bits)
    for i in range(grp):
        wst_re_ref[i * CHUNK:(i + 1) * CHUNK, :] = (er_re * bbr[i:i + 1] - er_im * bbi[i:i + 1]).astype(BF16)
        wst_im_ref[i * CHUNK:(i + 1) * CHUNK, :] = (er_re * bbi[i:i + 1] + er_im * bbr[i:i + 1]).astype(BF16)

    e1_re = e_re * ab_s[0] - e_im * ab_s[1]
    e1_im = e_re * ab_s[1] + e_im * ab_s[0]
    cr_t, ci_t = ct_ref[gi, 0], ct_ref[gi, 1]
    for o in range(grp):
        wa_ref[:, o * CHUNK:(o + 1) * CHUNK] = (cr_t[:, o:o + 1] * e1_re - ci_t[:, o:o + 1] * e1_im).astype(BF16)
        wb_ref[:, o * CHUNK:(o + 1) * CHUNK] = (-(cr_t[:, o:o + 1] * e1_im + ci_t[:, o:o + 1] * e1_re)).astype(BF16)

    ac_re, ac_im = cpow(ab_l, jnp.full((1, S5_STATE), CHUNK, jnp.int32), bits)
    ac_ref[0:1, :] = ac_re
    ac_ref[1:2, :] = ac_im


S5_K_SLICE = 2


def _s5_kernel(*refs, batch):
    cur, nxt = refs[:S5_PARAMS], refs[S5_PARAMS:2 * S5_PARAMS]
    d_ref, u_ref, o_ref, kmat_ref = refs[2 * S5_PARAMS:2 * S5_PARAMS + 4]
    rest = refs[2 * S5_PARAMS + 4:]
    slots = [rest[6 * i:6 * i + 6] for i in range(S5_PAIR)]
    ybig_ref, sre_ref, sim_ref, hre_ref, him_ref = rest[6 * S5_PAIR:]
    grp = S5_GROUP
    nc = u_ref.shape[1]
    nck = nc // batch

    @pl.when(pl.program_id(0) == 0)
    def _():
        _s5_prepare(cur, 0, slots[0], kmat_ref)

        def rows(ci, carry):
            _s5_toeplitz_rows(kmat_ref, slots[0][0], ci)
            return carry
        lax.fori_loop(0, grp, rows, 0)

    for gi in range(S5_PAIR):
        m_ref, wst_re_ref, wst_im_ref, wa_ref, wb_ref, ac_ref = slots[gi]
        following = slots[(gi + 1) % S5_PAIR]
        lhs = jnp.concatenate([u_ref[gi * grp + i].astype(BF16) for i in range(grp)], axis=1)
        sre_ref[...] = jnp.dot(lhs, wst_re_ref[...], preferred_element_type=F32)
        sim_ref[...] = jnp.dot(lhs, wst_im_ref[...], preferred_element_type=F32)
        if gi + 1 < S5_PAIR:
            _s5_prepare(cur, gi + 1, following, kmat_ref)
        else:
            _s5_prepare(nxt, 0, following, kmat_ref)
        ybig_ref[...] = jnp.zeros(ybig_ref.shape, F32)

        def contract(it, carry, gi=gi, m_ref=m_ref, following=following):
            chans = [gi * grp + it * S5_K_SLICE + j for j in range(S5_K_SLICE)]
            part = jnp.concatenate([u_ref[ch].astype(BF16) for ch in chans], axis=1)
            k0 = pl.multiple_of(it * (S5_K_SLICE * CHUNK), S5_K_SLICE * CHUNK)
            ybig_ref[...] += jnp.dot(part, m_ref[pl.ds(k0, S5_K_SLICE * CHUNK), :], preferred_element_type=F32)
            for j in range(S5_K_SLICE):
                _s5_toeplitz_rows(kmat_ref, following[0], it * S5_K_SLICE + j)
            return carry
        lax.fori_loop(0, grp // S5_K_SLICE, contract, 0)

        ac_re, ac_im = ac_ref[0:1, :], ac_ref[1:2, :]

        def carry_state(k, carry, ac_re=ac_re, ac_im=ac_im):
            new = []
            for b in range(batch):
                h_re, h_im = carry[2 * b], carry[2 * b + 1]
                row = b * nck + k
                hre_ref[pl.ds(row, 1), :] = h_re
                him_ref[pl.ds(row, 1), :] = h_im
                new.append(ac_re * h_re - ac_im * h_im + sre_ref[pl.ds(row, 1), :])
                new.append(ac_re * h_im + ac_im * h_re + sim_ref[pl.ds(row, 1), :])
            return tuple(new)

        zero = jnp.zeros((1, S5_STATE), F32)
        lax.fori_loop(0, nck, carry_state, (zero,) * (2 * batch))

        y = (ybig_ref[...]
             + jnp.dot(hre_ref[...].astype(BF16), wa_ref[...], preferred_element_type=F32)
             + jnp.dot(him_ref[...].astype(BF16), wb_ref[...], preferred_element_type=F32))
        for o in range(grp):
            ch = gi * grp + o
            o_ref[ch] = jax.nn.gelu(y[:, o * CHUNK:(o + 1) * CHUNK] + d_ref[ch] * u_ref[ch])


def s5_core(u3, d_skip, log_step, lam_re, lam_im, b_re, b_im, c_re, c_im, batch):
    width, nc, _ = u3.shape
    tokens = nc * CHUNK
    groups = width // S5_GROUP
    assert groups % S5_PAIR == 0
    steps = groups // S5_PAIR
    ls = log_step.reshape(groups, 1, 1)
    lam_l = jnp.stack([lam_re, lam_im], axis=1)
    lam_s = jnp.stack([lam_re, lam_im], axis=2)
    b_t = jnp.stack([b_re, b_im], axis=1).transpose(0, 1, 3, 2)
    c = jnp.stack([c_re, c_im], axis=1)
    c_t = c.transpose(0, 1, 3, 2)
    params = (ls, lam_l, lam_s, b_t, c, c_t)
    gw = S5_GROUP * CHUNK
    cur = lambda a: pl.BlockSpec((S5_PAIR,) + a.shape[1:], lambda g: (g,) + (0,) * (a.ndim - 1))
    nxt = lambda a: pl.BlockSpec((1,) + a.shape[1:],
                                 lambda g: (jnp.minimum(S5_PAIR * (g + 1), groups - 1),) + (0,) * (a.ndim - 1))
    chan = pl.BlockSpec((S5_PAIR * S5_GROUP, nc, CHUNK), lambda g: (g, 0, 0))
    slot = [pltpu.VMEM((gw, gw), BF16),
            pltpu.VMEM((gw, S5_STATE), BF16), pltpu.VMEM((gw, S5_STATE), BF16),
            pltpu.VMEM((S5_STATE, gw), BF16), pltpu.VMEM((S5_STATE, gw), BF16),
            pltpu.VMEM((2, S5_STATE), F32)]
    out = pl.pallas_call(
        functools.partial(_s5_kernel, batch=batch), grid=(steps,),
        in_specs=[cur(a) for a in params] + [nxt(a) for a in params]
        + [pl.BlockSpec((S5_PAIR * S5_GROUP, 1, 1), lambda g: (g, 0, 0)), chan],
        out_specs=chan,
        out_shape=jax.ShapeDtypeStruct((width, nc, CHUNK), F32),
        scratch_shapes=[pltpu.VMEM((S5_GROUP * S5_GROUP, CHUNK), F32)] + slot * S5_PAIR
        + [pltpu.VMEM((nc, gw), F32)] + [pltpu.VMEM((nc, S5_STATE), F32)] * 4,
        compiler_params=_params("arbitrary"), name="s5_core")(*params, *params, d_skip.reshape(width, 1, 1), u3)
    return out.reshape(width, tokens)


def _glu_t_kernel(y_ref, z_ref, w_ref, b_ref, o_ref):
    y = y_ref[...]
    acc = jnp.dot(w_ref[...], y.astype(BF16), preferred_element_type=F32)
    z = z_ref[...].astype(F32)
    o_ref[...] = (y * jax.nn.sigmoid(acc + b_ref[...]) * (z * jax.nn.sigmoid(z))).astype(o_ref.dtype)


def s5_glu_t(y_t, z_t, w_glu_t, b_glu, tm=256):
    width, tokens = y_t.shape
    tm = _tile(tokens, tm)
    col = lambda i: (0, i)
    return pl.pallas_call(
        _glu_t_kernel, grid=(tokens // tm,),
        in_specs=[pl.BlockSpec((width, tm), col), pl.BlockSpec((width, tm), col),
                  pl.BlockSpec((width, width), lambda i: (0, 0)),
                  pl.BlockSpec((width, 1), lambda i: (0, 0))],
        out_specs=pl.BlockSpec((width, tm), col),
        out_shape=jax.ShapeDtypeStruct((width, tokens), BF16),
        compiler_params=_params("parallel"), name="s5_glu")(
            y_t, z_t, w_glu_t, b_glu.reshape(width, 1))


def _proj_conv_kernel(x_ref, wh_ref, wg_ref, wc_ref, wz_ref, cw_ref, cb_ref, o_ref, tail_ref, *, seq):
    tm = x_ref.shape[0]
    x = x_ref[...]
    v = (jnp.dot(x, wc_ref[...], preferred_element_type=F32) * jnp.dot(x, wh_ref[...], preferred_element_type=F32))
    first = (pl.program_id(1) * tm) % seq == 0
    ext = jnp.concatenate([jnp.where(first, 0.0, tail_ref[...]), v], axis=0)
    conv = (cw_ref[0:1, :] * ext[6:6 + tm] + cw_ref[1:2, :] * ext[7:7 + tm] + cw_ref[2:3, :] * v
            + cb_ref[...])
    tail_ref[...] = v[tm - 8:]
    z = jnp.dot(x, wz_ref[...], preferred_element_type=F32)
    o_ref[...] = (jnp.dot(x, wg_ref[...], preferred_element_type=F32) * conv
                  * (z * jax.nn.sigmoid(z))).astype(o_ref.dtype)


def proj_short_conv(x, w_in, col0, conv_w, conv_b, seq, tm=1024, tc=256):
    tokens, k = x.shape
    width = conv_w.shape[1]
    tm, tc = _tile(seq, tm), _tile(width, tc)
    ncb = width // tc
    seg = lambda s: pl.BlockSpec((k, tc), lambda j, i: (0, (col0 + s * width) // tc + j))
    return pl.pallas_call(
        functools.partial(_proj_conv_kernel, seq=seq), grid=(ncb, tokens // tm),
        in_specs=[pl.BlockSpec((tm, k), lambda j, i: (i, 0)), seg(0), seg(1), seg(2), seg(3),
                  pl.BlockSpec((CONV_K, tc), lambda j, i: (0, j)),
                  pl.BlockSpec((1, tc), lambda j, i: (0, j))],
        out_specs=pl.BlockSpec((tm, tc), lambda j, i: (i, j)),
        out_shape=jax.ShapeDtypeStruct((tokens, width), BF16),
        scratch_shapes=[pltpu.VMEM((8, tc), F32)],
        compiler_params=_params("parallel", "arbitrary"), name="proj_short_conv")(
            x, w_in, w_in, w_in, w_in, conv_w, conv_b.reshape(1, width))


def _out0_kernel(ya_ref, yb_ref, wa_ref, wb_ref, o_ref):
    acc = lax.dot_general(ya_ref[...], wa_ref[...], (((0,), (0,)), ((), ())), preferred_element_type=F32)
    o_ref[...] = acc + jnp.dot(yb_ref[...], wb_ref[...], preferred_element_type=F32)


def out_proj0(ya_t, yb, w_out, tm=1024, tn=1024):
    wa_rows, tokens = ya_t.shape
    wb_rows = yb.shape[1]
    n = w_out.shape[1]
    tm, tn = _tile(tokens, tm), _tile(n, tn)
    return pl.pallas_call(
        _out0_kernel, grid=(tokens // tm, n // tn),
        in_specs=[pl.BlockSpec((wa_rows, tm), lambda i, j: (0, i)),
                  pl.BlockSpec((tm, wb_rows), lambda i, j: (i, 0)),
                  pl.BlockSpec((wa_rows, tn), lambda i, j: (0, j)),
                  pl.BlockSpec((wb_rows, tn), lambda i, j: (wa_rows // wb_rows, j))],
        out_specs=pl.BlockSpec((tm, tn), lambda i, j: (i, j)),
        out_shape=jax.ShapeDtypeStruct((tokens, n), F32),
        compiler_params=_params("parallel", "parallel"), name="out_proj0")(ya_t, yb, w_out, w_out)


def _ln_kernel(h_ref, mix_ref, g_ref, b_ref, o_ref, ob_ref):
    y = DEEPNORM_ALPHA * h_ref[...] + mix_ref[...]
    mu = jnp.mean(y, axis=-1, keepdims=True)
    yc = y - mu
    var = jnp.mean(yc * yc, axis=-1, keepdims=True)
    out = yc * lax.rsqrt(var + LN_EPS) * g_ref[...] + b_ref[...]
    o_ref[...] = out
    ob_ref[...] = out.astype(BF16)


def deepnorm_ln(h, mix, g, b, tm=256):
    tokens, d = h.shape
    tm = _tile(tokens, tm)
    row = pl.BlockSpec((tm, d), lambda i: (i, 0))
    vec = pl.BlockSpec((1, d), lambda i: (0, 0))
    return pl.pallas_call(
        _ln_kernel, grid=(tokens // tm,),
        in_specs=[row, row, vec, vec], out_specs=[row, row],
        out_shape=[jax.ShapeDtypeStruct((tokens, d), F32), jax.ShapeDtypeStruct((tokens, d), BF16)],
        compiler_params=_params("parallel"), name="deepnorm_ln")(h, mix, g.reshape(1, d), b.reshape(1, d))


EPILOGUE_SPLIT = 4


def _gate_kernel(hb_ref, w_ref, bg_ref, h_ref, p_ref, wp_ref, o_ref, ob_ref):
    slab = hb_ref.shape[0] // EPILOGUE_SPLIT
    for s in range(EPILOGUE_SPLIT):
        rows = slice(s * slab, (s + 1) * slab)
        gate = jax.nn.sigmoid(jnp.dot(hb_ref[rows, :], w_ref[...], preferred_element_type=F32) + bg_ref[...])
        out = h_ref[rows, :] + gate * jnp.dot(p_ref[rows, :], wp_ref[...], preferred_element_type=F32)
        o_ref[rows, :] = out
        ob_ref[rows, :] = out.astype(BF16)


def ple_gate(h, hb, w_gate, b_gate, p, w_ple, tm=1024, tn=512):
    tokens, d = h.shape
    pd = p.shape[1]
    tm, tn = _tile(tokens, tm), _tile(d, tn)
    return pl.pallas_call(
        _gate_kernel, grid=(tokens // tm, d // tn),
        in_specs=[pl.BlockSpec((tm, d), lambda i, j: (i, 0)),
                  pl.BlockSpec((d, tn), lambda i, j: (0, j)),
                  pl.BlockSpec((1, tn), lambda i, j: (0, j)),
                  pl.BlockSpec((tm, tn), lambda i, j: (i, j)),
                  pl.BlockSpec((tm, pd), lambda i, j: (i, 0)),
                  pl.BlockSpec((pd, tn), lambda i, j: (0, j))],
        out_specs=[pl.BlockSpec((tm, tn), lambda i, j: (i, j))] * 2,
        out_shape=[jax.ShapeDtypeStruct((tokens, d), F32), jax.ShapeDtypeStruct((tokens, d), BF16)],
        compiler_params=_params("parallel", "parallel"), name="ple_gate")(
            hb, w_gate, b_gate.reshape(1, d), h, p, w_ple)


def _distance_tables(dil):
    groups = DEINT // dil
    rows = ATT_GROUP_ROWS[dil]
    f_q, a_q = np.divmod(np.arange(groups * rows), rows)
    iq = groups * a_q + f_q

    def table(k_rows, a0):
        f_k, a_k = np.divmod(np.arange(groups * k_rows), k_rows)
        delta = iq[:, None] - (groups * (a_k + a0) + f_k)[None, :]
        return np.where((delta >= 0) & (delta <= SPAN), dil * delta, FAR).astype(np.float32)

    return table(2 * rows, -rows), table(rows, 0)


def _attn_kernel(sl_ref, *refs):
    nb = len(DILATED)
    dist_refs, unperm_ref = refs[:2 * nb], refs[2 * nb]
    q_ref, k_ref, v_ref, z_ref, o_ref, m_ref, l_ref, acc_ref = refs[2 * nb + 1:2 * nb + 9]
    bias_refs = refs[2 * nb + 9:]
    n_res = q_ref.shape[1]
    slope = sl_ref[0]
    for d_ref, b_ref in zip(dist_refs, bias_refs):
        b_ref[...] = -slope * d_ref[...]

    def rows_of(ref, res, starts, size):
        if len(ref.shape) == 3:
            return jnp.concatenate([ref[r, pl.ds(s, size), :] for r, s in zip(res, starts)], axis=0)
        return jnp.concatenate([ref[pl.ds(r * n_res + s, size), :] for r, s in zip(res, starts)], axis=0)

    def block_group(dil, items, tabs, fresh, final):
        groups, rows = DEINT // dil, ATT_GROUP_ROWS[dil]
        align = min(rows, 16)
        res = [[dil * f + c for f in range(groups)] for c, _, _ in items]
        q_at = [[pl.multiple_of(rows * j, align)] * groups for _, j, _ in items]
        k_at = [qa if first else [pl.multiple_of(rows * (j - 1), align)] * groups
                for qa, (_, j, first) in zip(q_at, items)]
        k_rows = [rows if first else 2 * rows for _, _, first in items]
        bias = [tabs[1] if first else tabs[0] for _, _, first in items]
        n = range(len(items))
        ones = jnp.ones((2 * rows * groups, HEAD_DIM), BF16)
        q = [rows_of(q_ref, res[i], q_at[i], rows) for i in n]
        k = [rows_of(k_ref, res[i], k_at[i], k_rows[i]) for i in n]
        v1 = [jnp.concatenate([rows_of(v_ref, res[i], k_at[i], k_rows[i]), ones[:k_rows[i] * groups]], axis=1)
              for i in n]
        if not fresh:
            m_prev = [rows_of(m_ref, res[i], q_at[i], rows) for i in n]
            l_prev = [rows_of(l_ref, res[i], q_at[i], rows) for i in n]
            acc_prev = [rows_of(acc_ref, res[i], q_at[i], rows) for i in n]
        s = [lax.dot_general(q[i], k[i], (((1,), (1,)), ((), ())), preferred_element_type=F32)
             * (HEAD_DIM ** -0.5) + bias[i][...] for i in n]
        m_new = [jnp.broadcast_to(jnp.max(s[i], axis=-1, keepdims=True), (rows * groups, HEAD_DIM)) for i in n]
        if not fresh:
            m_new = [jnp.maximum(m_prev[i], m_new[i]) for i in n]
        p = [jnp.concatenate([jnp.exp(s[i][:, t:t + HEAD_DIM] - m_new[i])
                              for t in range(0, s[i].shape[1], HEAD_DIM)], axis=1).astype(BF16) for i in n]
        pv = [jnp.dot(p[i], v1[i], preferred_element_type=F32) for i in n]
        acc_new = [pv[i][:, :HEAD_DIM] for i in n]
        l_new = [pv[i][:, HEAD_DIM:] for i in n]
        if not fresh:
            alpha = [jnp.exp(m_prev[i] - m_new[i]) for i in n]
            l_new = [alpha[i] * l_prev[i] + l_new[i] for i in n]
            acc_new = [alpha[i] * acc_prev[i] + acc_new[i] for i in n]
        for i in n:
            if final:
                span = rows * groups
                at = pl.multiple_of(span * items[i][1], span)
                o = jnp.dot(unperm_ref[...], (acc_new[i] / l_new[i]).astype(BF16), preferred_element_type=F32)
                z = z_ref[pl.ds(at, span), :].astype(F32)
                o_ref[pl.ds(at, span), :] = (o * (z * jax.nn.sigmoid(z))).astype(o_ref.dtype)
                continue
            for f in range(groups):
                at = res[i][f] * n_res + q_at[i][f]
                part = slice(f * rows, (f + 1) * rows)
                m_ref[pl.ds(at, rows), :] = m_new[i][part]
                l_ref[pl.ds(at, rows), :] = l_new[i][part]
                acc_ref[pl.ds(at, rows), :] = acc_new[i][part]

    order = sorted(range(nb), key=lambda i: -DILATED[i][1])
    for pos, idx in enumerate(order):
        dil = DILATED[idx][1]
        fresh, final = pos == 0, pos == nb - 1
        tabs = (bias_refs[2 * idx], bias_refs[2 * idx + 1])
        n_blocks = n_res // ATT_GROUP_ROWS[dil]
        per = min(ATT_BLOCKS_PER_STEP[dil], n_blocks)
        subs = min(ATT_BLOCKS_PER_STEP[dil] // per, dil)
        assert n_blocks % per == 0 and dil % subs == 0

        def subsequences(cc, carry, dil=dil, fresh=fresh, final=final, tabs=tabs, n_blocks=n_blocks, per=per,
                         subs=subs):
            cs = [cc * subs + i for i in range(subs)]
            block_group(dil, [(c, j, j == 0) for c in cs for j in range(per)], tabs, fresh, final)
            if n_blocks > per:
                def later(g, carry2):
                    block_group(dil, [(c, g * per + j, False) for c in cs for j in range(per)], tabs, fresh, final)
                    return carry2
                lax.fori_loop(1, n_blocks // per, later, 0)
            return carry
        lax.fori_loop(0, dil // subs, subsequences, 0)


def dilated_attention(qkv, z, batch, seq):
    heads, tokens, _ = z.shape
    n_res = seq // DEINT
    slopes = jnp.asarray(np.exp2(-8.0 * np.arange(1, heads + 1) / heads).astype(np.float32)).reshape(heads, 1, 1)
    tables = [jnp.asarray(t) for _, dil in DILATED for t in _distance_tables(dil)]
    assert min(d for _, d in DILATED) == 1
    rows = ATT_GROUP_ROWS[1]
    tok = np.arange(DEINT * rows)
    unperm = np.zeros((DEINT * rows, DEINT * rows), np.float32)
    unperm[tok, (tok % DEINT) * rows + tok // DEINT] = 1.0
    whole = lambda t: pl.BlockSpec(t.shape, lambda b, h: (0, 0))
    grouped = lambda off: pl.BlockSpec((DEINT, None, n_res, HEAD_DIM), lambda b, h: (b, off + h, 0, 0))
    natural = pl.BlockSpec((None, seq, HEAD_DIM), lambda b, h: (h, b, 0))
    return pl.pallas_call(
        _attn_kernel, grid=(batch, heads),
        in_specs=[pl.BlockSpec((1, 1, 1), lambda b, h: (h, 0, 0))] + [whole(t) for t in tables]
        + [whole(unperm), grouped(0), grouped(heads), grouped(2 * heads), natural],
        out_specs=natural,
        out_shape=jax.ShapeDtypeStruct((heads, tokens, HEAD_DIM), BF16),
        scratch_shapes=[pltpu.VMEM((seq, HEAD_DIM), F32)] * 3
        + [pltpu.VMEM(t.shape, F32) for t in tables],
        compiler_params=_params("parallel", "parallel"),
        name="dilated_attn")(slopes, *tables, jnp.asarray(unperm, BF16), qkv, qkv, qkv, z)


def kernel(x, p, ab_w_in, ab_lambda_re, ab_lambda_im, ab_log_step, ab_b_re, ab_b_im, ab_c_re, ab_c_im, ab_d,
           ab_w_glu, ab_b_glu, ab_conv_w, ab_conv_b, ab_w_out, at_w_in, at_w_out, ln_g, ln_b, ple_w,
           ple_gate_w, ple_gate_b):
    batch, seq, d = x.shape
    tokens = batch * seq
    s5w = d // 2
    assert seq % (DEINT * SPAN) == 0 and seq % CHUNK == 0 and s5w % S5_GROUP == 0
    h = x.reshape(tokens, d)
    hb = h.astype(BF16)
    pb = p.reshape(DEPTH, tokens, PLE_DIM).astype(BF16)

    w_in = ab_w_in[0].astype(BF16)
    w_t = w_in[:, :2 * s5w].T
    u3 = matmul_nt(w_t, hb, 0, s5w, F32, chunked=True)
    za_t = matmul_nt(w_t, hb, s5w, s5w, BF16)
    y_t = s5_core(u3, ab_d[0], ab_log_step[0], ab_lambda_re[0], ab_lambda_im[0], ab_b_re[0], ab_b_im[0],
                  ab_c_re[0], ab_c_im[0], batch)
    ya_t = s5_glu_t(y_t, za_t, ab_w_glu[0].T.astype(BF16), ab_b_glu[0])
    yb = proj_short_conv(hb, w_in, 2 * s5w, ab_conv_w[0], ab_conv_b[0], seq)
    mix = out_proj0(ya_t, yb, ab_w_out[0].astype(BF16))
    h, hb = deepnorm_ln(h, mix, ln_g[0], ln_b[0])
    h, hb = ple_gate(h, hb, ple_gate_w[0].astype(BF16), ple_gate_b[0], pb[0], ple_w[0].astype(BF16))

    w_in = at_w_in[0].astype(BF16)
    qkv = matmul_deinterleaved(hb, w_in, 3 * d, BF16, seq)
    z = matmul_heads_out(hb, w_in, 3 * d, d, BF16)
    og = dilated_attention(qkv, z, batch, seq)
    mix = matmul_heads_in(og, at_w_out[0].astype(BF16), F32)
    h, hb = deepnorm_ln(h, mix, ln_g[1], ln_b[1])
    h, _ = ple_gate(h, hb, ple_gate_w[1].astype(BF16), ple_gate_b[1], pb[1], ple_w[1].astype(BF16))
    return h.reshape(batch, seq, d)
```
